```python
import math
import jax, jax.numpy as jnp
from jax import lax
import numpy as np

D_MODEL = 1024
BATCH = 4
SEQ = 8192
DEPTH = 1

D_MIX = D_MODEL
DA_HEADS = 4
DA_HEAD_DIM = 64
DA_VDIM = 2 * DA_HEAD_DIM
DA_WIDTH = DA_HEADS * DA_VDIM
Q_BLOCK = 128
ROPE_THETA = 10000.0
LAMBDA_STD = 0.1
GM_HEADS = 4
GM_HEAD_DIM = 128
GM_WIDTH = GM_HEADS * GM_HEAD_DIM
GM_CHUNK = 128
IN_COLS = 3 * DA_WIDTH + 2 * GM_WIDTH
N_GROUPS = 4
EXPERTS_PER_GROUP = 8
N_EXPERTS = N_GROUPS * EXPERTS_PER_GROUP
TOP_K = 2
D_EXPERT = 512
MOE_BLOCK = 128
PLE_DIM = 256
EPS = 1e-6

kernel_name = "hybrid_diffattn_gmlp_hmoe_block"


def rms_norm(x, g):
    xf = x.astype(jnp.float32)
    y = xf * lax.rsqrt(jnp.mean(xf * xf, axis=-1, keepdims=True) + EPS)
    return (y * g.astype(jnp.float32)).astype(x.dtype)


def layer_norm(x, g, b):
    xf = x.astype(jnp.float32)
    mu = jnp.mean(xf, axis=-1, keepdims=True)
    xc = xf - mu
    y = xc * lax.rsqrt(jnp.mean(xc * xc, axis=-1, keepdims=True) + EPS)
    return (y * g.astype(jnp.float32) + b.astype(jnp.float32)).astype(x.dtype)


def rope_tables(seq, dim):
    inv = 1.0 / (ROPE_THETA ** (jnp.arange(0, dim, 2, dtype=jnp.float32) / dim))
    ang = jnp.arange(seq, dtype=jnp.float32)[:, None] * inv[None, :]
    ang = jnp.concatenate([ang, ang], axis=-1)
    return jnp.cos(ang), jnp.sin(ang)


def apply_rope(x, cos, sin):
    half = x.shape[-1] // 2
    x1, x2 = x[..., :half], x[..., half:]
    rot = jnp.concatenate([-x2, x1], axis=-1)
    c = cos[:, None, None, :]
    s = sin[:, None, None, :]
    return (x * c + rot * s).astype(x.dtype)


def diff_attention(q, k, v, lq1, lk1, lq2, lk2, subln_g, lambda_init):
    B, S, _ = q.shape
    nq = S // Q_BLOCK
    cos, sin = rope_tables(S, DA_HEAD_DIM)
    q = apply_rope(q.reshape(B, S, DA_HEADS, 2, DA_HEAD_DIM), cos, sin)
    k = apply_rope(k.reshape(B, S, DA_HEADS, 2, DA_HEAD_DIM), cos, sin)
    v = v.reshape(B, S, DA_HEADS, DA_VDIM).transpose(0, 2, 1, 3)
    kt = k.transpose(0, 2, 3, 1, 4)
    qb = q.reshape(B, nq, Q_BLOCK, DA_HEADS, 2, DA_HEAD_DIM).transpose(1, 0, 3, 4, 2, 5)
    lam = (jnp.exp(jnp.sum(lq1.astype(jnp.float32) * lk1.astype(jnp.float32)))
           - jnp.exp(jnp.sum(lq2.astype(jnp.float32) * lk2.astype(jnp.float32)))
           + lambda_init)
    scale = 1.0 / math.sqrt(DA_HEAD_DIM)
    key_pos = jnp.arange(S)

    def block(args):
        q_blk, c = args
        s = jnp.einsum('bhmqd,bhmkd->bhmqk', q_blk, kt).astype(jnp.float32) * scale
        q_pos = c * Q_BLOCK + jnp.arange(Q_BLOCK)
        mask = key_pos[None, :] <= q_pos[:, None]
        s = jnp.where(mask, s, -jnp.inf)
        a = jax.nn.softmax(s, axis=-1)
        amap = a[:, :, 0] - lam * a[:, :, 1]
        return jnp.einsum('bhqk,bhkd->bhqd', amap.astype(v.dtype), v)

    o = lax.map(block, (qb, jnp.arange(nq)))
    o = o.transpose(1, 0, 3, 2, 4)
    o = rms_norm(o, subln_g) * (1.0 - lambda_init)
    return o.reshape(B, S, DA_WIDTH).astype(q.dtype)


def gmlp_spatial(z, ln_g, ln_b, ws, bs, out_g):
    B, S, _ = z.shape
    nc = S // GM_CHUNK
    a = jax.nn.gelu(z)
    u, vv = a[..., :GM_WIDTH], a[..., GM_WIDTH:]
    vv = layer_norm(vv, ln_g, ln_b)
    vv = vv.reshape(B, nc, GM_CHUNK, GM_HEADS, GM_HEAD_DIM)
    causal = jnp.tril(jnp.ones((GM_CHUNK, GM_CHUNK), dtype=bool))
    w = jnp.where(causal[None], ws, jnp.zeros_like(ws))
    mixed = jnp.einsum('hts,bcshd->bcthd', w, vv) + bs.T[:, :, None]
    out = u.reshape(B, nc, GM_CHUNK, GM_HEADS, GM_HEAD_DIM) * mixed
    return rms_norm(out.reshape(B, S, GM_WIDTH), out_g)


def hier_moe(h, w_grp, w_exp, w_g, w_u, w_d):
    B, S, D = h.shape
    T = B * S
    t = h.reshape(T, D)
    tok = jnp.arange(T)
    grp_logits = (t @ w_grp).astype(jnp.float32)
    grp_prob = jax.nn.softmax(grp_logits, axis=-1)
    g_idx = jnp.argmax(grp_logits, axis=-1)
    g_gate = grp_prob[tok, g_idx]
    exp_logits = (t @ w_exp).astype(jnp.float32).reshape(T, N_GROUPS, EXPERTS_PER_GROUP)
    in_grp = exp_logits[tok, g_idx]
    top_val, top_loc = lax.top_k(in_grp, TOP_K)
    gate = jax.nn.softmax(top_val, axis=-1) * g_gate[:, None]
    e_id = g_idx[:, None] * EXPERTS_PER_GROUP + top_loc

    n_assign = T * TOP_K
    flat_e = e_id.reshape(-1)
    flat_w = gate.reshape(-1)
    flat_tok = jnp.repeat(tok, TOP_K)
    order = jnp.argsort(flat_e)
    se, stok, sw = flat_e[order], flat_tok[order], flat_w[order]
    counts = jnp.bincount(flat_e, length=N_EXPERTS)
    start = jnp.cumsum(counts) - counts
    padded = ((counts + MOE_BLOCK - 1) // MOE_BLOCK) * MOE_BLOCK
    pend = jnp.cumsum(padded)
    pstart = pend - padded
    dest = pstart[se] + (jnp.arange(n_assign) - start[se])
    n_rows = n_assign + N_EXPERTS * MOE_BLOCK
    n_blocks = n_rows // MOE_BLOCK
    buf = jnp.zeros((n_rows, D), t.dtype).at[dest].set(t[stok])
    blk_exp = jnp.clip(jnp.searchsorted(pend, jnp.arange(n_blocks) * MOE_BLOCK, side='right'),
                       0, N_EXPERTS - 1)

    def expert_block(args):
        xb, e = args
        return (jax.nn.silu(xb @ w_g[e]) * (xb @ w_u[e])) @ w_d[e]

    ybuf = lax.map(expert_block, (buf.reshape(n_blocks, MOE_BLOCK, D), blk_exp)).reshape(n_rows, D)
    y = ybuf[dest] * sw[:, None].astype(t.dtype)
    out = jnp.zeros((T, D), t.dtype).at[stok].add(y)
    return out.reshape(B, S, D)


def setup_inputs(seed: int = 0) -> dict:
    key = jax.random.key(seed)
    ks = jax.random.split(key, 26)
    f32 = jnp.float32
    nrm = lambda k, shape, s: (jax.random.normal(k, shape, f32) * s)
    gain = lambda k, shape: 1.0 + 0.02 * jax.random.normal(k, shape, f32)
    L, D = DEPTH, D_MODEL
    return {
        "x": jax.random.normal(ks[0], (BATCH, SEQ, D), f32),
        "p": jax.random.normal(ks[1], (DEPTH, BATCH, SEQ, PLE_DIM), f32),
        "attn_norm": gain(ks[2], (L, D)),
        "w_in": nrm(ks[3], (L, D, IN_COLS), D ** -0.5),
        "lambda_q1": nrm(ks[4], (L, DA_HEAD_DIM), LAMBDA_STD),
        "lambda_k1": nrm(ks[5], (L, DA_HEAD_DIM), LAMBDA_STD),
        "lambda_q2": nrm(ks[6], (L, DA_HEAD_DIM), LAMBDA_STD),
        "lambda_k2": nrm(ks[7], (L, DA_HEAD_DIM), LAMBDA_STD),
        "diff_subln": gain(ks[8], (L, DA_VDIM)),
        "gm_ln_gain": gain(ks[9], (L, GM_WIDTH)),
        "gm_ln_bias": nrm(ks[10], (L, GM_WIDTH), 0.02),
        "gm_spatial_w": nrm(ks[11], (L, GM_HEADS, GM_CHUNK, GM_CHUNK), GM_CHUNK ** -0.5),
        "gm_spatial_b": gain(ks[12], (L, GM_HEADS, GM_CHUNK)),
        "gm_out_norm": gain(ks[13], (L, GM_WIDTH)),
        "w_out": nrm(ks[14], (L, DA_WIDTH + GM_WIDTH, D), (DA_WIDTH + GM_WIDTH) ** -0.5),
        "moe_norm": gain(ks[15], (L, D)),
        "w_group_router": nrm(ks[16], (L, D, N_GROUPS), D ** -0.5),
        "w_expert_router": nrm(ks[17], (L, D, N_EXPERTS), D ** -0.5),
        "w_expert_gate": nrm(ks[18], (L, N_EXPERTS, D, D_EXPERT), D ** -0.5),
        "w_expert_up": nrm(ks[19], (L, N_EXPERTS, D, D_EXPERT), D ** -0.5),
        "w_expert_down": nrm(ks[20], (L, N_EXPERTS, D_EXPERT, D), D_EXPERT ** -0.5),
        "ple_norm": gain(ks[21], (L, D)),
        "w_ple_gate": nrm(ks[22], (L, D, D), D ** -0.5),
        "b_ple_gate": nrm(ks[23], (L, D), 0.02),
        "w_ple_proj": nrm(ks[24], (L, PLE_DIM, D), PLE_DIM ** -0.5),
        "final_norm": gain(ks[25], (D,)),
    }


def reference(x, p, attn_norm, w_in, lambda_q1, lambda_k1, lambda_q2, lambda_k2, diff_subln,
              gm_ln_gain, gm_ln_bias, gm_spatial_w, gm_spatial_b, gm_out_norm, w_out,
              moe_norm, w_group_router, w_expert_router, w_expert_gate, w_expert_up,
              w_expert_down, ple_norm, w_ple_gate, b_ple_gate, w_ple_proj, final_norm):
    for i in range(DEPTH):
        lambda_init = 0.8 - 0.6 * math.exp(-0.3 * i)
        h = rms_norm(x, attn_norm[i])
        z = h @ w_in[i]
        q = z[..., :DA_WIDTH]
        k = z[..., DA_WIDTH:2 * DA_WIDTH]
        v = z[..., 2 * DA_WIDTH:3 * DA_WIDTH]
        zg = z[..., 3 * DA_WIDTH:]
        o_da = diff_attention(q, k, v, lambda_q1[i], lambda_k1[i], lambda_q2[i], lambda_k2[i],
                              diff_subln[i], lambda_init)
        o_gm = gmlp_spatial(zg, gm_ln_gain[i], gm_ln_bias[i], gm_spatial_w[i],
                            gm_spatial_b[i], gm_out_norm[i])
        x = x + jnp.concatenate([o_da, o_gm], axis=-1) @ w_out[i]
        x = x + hier_moe(rms_norm(x, moe_norm[i]), w_group_router[i], w_expert_router[i],
                         w_expert_gate[i], w_expert_up[i], w_expert_down[i])
        gate = jax.nn.sigmoid(rms_norm(x, ple_norm[i]) @ w_ple_gate[i] + b_ple_gate[i])
        x = x + gate * (p[i] @ w_ple_proj[i])
    return rms_norm(x, final_norm)
```

```python
import functools
import math

import numpy as np
import jax
import jax.numpy as jnp
from jax import lax
from jax.experimental import pallas as pl
from jax.experimental.pallas import tpu as pltpu

F32 = jnp.float32
BF16 = jnp.bfloat16

DA_HEADS = 4
DA_HEAD_DIM = 64
DA_VDIM = 2 * DA_HEAD_DIM
DA_WIDTH = DA_HEADS * DA_VDIM
ROPE_THETA = 10000.0
GM_HEADS = 4
GM_HEAD_DIM = 128
GM_WIDTH = GM_HEADS * GM_HEAD_DIM
GM_CHUNK = 128
N_GROUPS = 4
EXPERTS_PER_GROUP = 8
N_EXPERTS = N_GROUPS * EXPERTS_PER_GROUP
TOP_K = 2
EPS = 1e-6

LANES = 128
ROPE_HALF = DA_HEAD_DIM // 2
NEG_BIG = -1e30

ROW_TILE = 256
ATT_TQ = 128
ATT_TK = 256
MOE_ROWS = 256
VMEM_LIMIT = 56 * 1024 * 1024


def _rms(x, g):
    return x * lax.rsqrt(jnp.mean(x * x, axis=-1, keepdims=True) + EPS) * g


def _gelu_tanh(x):
    c = math.sqrt(2.0 / math.pi)
    return x * (0.5 * (1.0 + jnp.tanh(c * (x + 0.044715 * (x * x * x)))))


def _in_proj_body(x_ref, g_ref, w_ref, cos_ref, sin_ref, lng_ref, lnb_ref, ws_ref, bst_ref, og_ref,
                  q_ref, k_ref, v_ref, ogm_ref, mix_ref):
    tm = x_ref.shape[0]
    h = _rms(x_ref[...], g_ref[...]).astype(BF16)
    cos = cos_ref[...]
    sin = sin_ref[...]

    zq = jnp.dot(h, w_ref[:, 0:DA_WIDTH], preferred_element_type=F32)
    zk = jnp.dot(h, w_ref[:, DA_WIDTH:2 * DA_WIDTH], preferred_element_type=F32)
    scale = 1.0 / math.sqrt(DA_HEAD_DIM)
    for hd in range(DA_HEADS):
        sl = slice(hd * LANES, (hd + 1) * LANES)
        qh = zq[:, sl]
        kh = zk[:, sl]
        q_ref[:, sl] = ((qh * cos + pltpu.roll(qh, LANES // 2, 1) * sin) * scale).astype(BF16)
        k_ref[:, sl] = (kh * cos + pltpu.roll(kh, LANES // 2, 1) * sin).astype(BF16)

    v_ref[...] = jnp.dot(h, w_ref[:, 2 * DA_WIDTH:3 * DA_WIDTH], preferred_element_type=F32).astype(BF16)

    a = _gelu_tanh(jnp.dot(h, w_ref[:, 3 * DA_WIDTH:3 * DA_WIDTH + 2 * GM_WIDTH], preferred_element_type=F32))
    u = a[:, :GM_WIDTH]
    vv = a[:, GM_WIDTH:]
    mu = jnp.mean(vv, axis=-1, keepdims=True)
    vc = vv - mu
    vln = vc * lax.rsqrt(jnp.mean(vc * vc, axis=-1, keepdims=True) + EPS) * lng_ref[...] + lnb_ref[...]
    vb = vln.astype(BF16)
    row = lax.broadcasted_iota(jnp.int32, (GM_CHUNK, GM_CHUNK), 0)
    col = lax.broadcasted_iota(jnp.int32, (GM_CHUNK, GM_CHUNK), 1)
    causal = row >= col
    for hd in range(GM_HEADS):
        wm = jnp.where(causal, ws_ref[hd], 0.0).astype(BF16)
        bias = bst_ref[:, hd:hd + 1]
        cs = slice(hd * GM_HEAD_DIM, (hd + 1) * GM_HEAD_DIM)
        for c in range(tm // GM_CHUNK):
            rs = slice(c * GM_CHUNK, (c + 1) * GM_CHUNK)
            mixed = jnp.dot(wm, vb[rs, cs], preferred_element_type=F32) + bias
            mix_ref[rs, cs] = u[rs, cs] * mixed
    ogm_ref[...] = _rms(mix_ref[...], og_ref[...]).astype(BF16)


def _in_proj(x2d, attn_norm, w_in_b, cos_t, sin_t, ln_g, ln_b, ws, bs_t, out_g, seq):
    t, d = x2d.shape
    tm = ROW_TILE
    n_pos = seq // tm
    row = lambda i: (i, 0)
    full2 = lambda i: (0, 0)
    return pl.pallas_call(
        _in_proj_body,
        grid=(t // tm,),
        in_specs=[
            pl.BlockSpec((tm, d), row),
            pl.BlockSpec((1, d), full2),
            pl.BlockSpec(w_in_b.shape, full2),
            pl.BlockSpec((tm, LANES), lambda i: (i % n_pos, 0)),
            pl.BlockSpec((tm, LANES), lambda i: (i % n_pos, 0)),
            pl.BlockSpec((1, GM_WIDTH), full2),
            pl.BlockSpec((1, GM_WIDTH), full2),
            pl.BlockSpec(ws.shape, lambda i: (0, 0, 0)),
            pl.BlockSpec(bs_t.shape, full2),
            pl.BlockSpec((1, GM_WIDTH), full2),
        ],
        out_specs=[
            pl.BlockSpec((tm, DA_WIDTH), row),
            pl.BlockSpec((tm, DA_WIDTH), row),
            pl.BlockSpec((tm, DA_WIDTH), row),
            pl.BlockSpec((tm, GM_WIDTH), row),
        ],
        out_shape=[
            jax.ShapeDtypeStruct((t, DA_WIDTH), BF16),
            jax.ShapeDtypeStruct((t, DA_WIDTH), BF16),
            jax.ShapeDtypeStruct((t, DA_WIDTH), BF16),
            jax.ShapeDtypeStruct((t, GM_WIDTH), BF16),
        ],
        scratch_shapes=[pltpu.VMEM((tm, GM_WIDTH), F32)],
        compiler_params=pltpu.CompilerParams(dimension_semantics=("parallel",), vmem_limit_bytes=VMEM_LIMIT),
        name="in_proj",
    )(x2d, attn_norm, w_in_b, cos_t, sin_t, ln_g, ln_b, ws, bs_t, out_g)


def _attn_body(lam_ref, g_ref, qt_ref, k_ref, vt_ref, o_ref, *, tq, tk, lambda_init):
    i = pl.program_id(2)
    lp = lam_ref[...]
    lam = (jnp.exp(jnp.sum(lp[0:1] * lp[1:2], axis=-1, keepdims=True))
           - jnp.exp(jnp.sum(lp[2:3] * lp[3:4], axis=-1, keepdims=True)) + lambda_init)

    qt = qt_ref[...]
    feat = lax.broadcasted_iota(jnp.int32, qt.shape, 0)
    is_map1 = (feat % DA_HEAD_DIM) < ROPE_HALF
    zero = jnp.zeros_like(qt)
    qst = jnp.concatenate([jnp.where(is_map1, qt, zero), jnp.where(is_map1, zero, qt)], axis=1)

    def step(j, carry, masked):
        m, l, acc = carry
        kj = k_ref[pl.ds(pl.multiple_of(j * tk, tk), tk), :]
        s = jnp.dot(kj, qst, preferred_element_type=F32)
        if masked:
            kpos = j * tk + lax.broadcasted_iota(jnp.int32, s.shape, 0)
            qpos = i * tq + lax.broadcasted_iota(jnp.int32, s.shape, 1) % tq
            s = jnp.where(kpos <= qpos, s, NEG_BIG)
        m_new = jnp.maximum(m, jnp.max(s, axis=0, keepdims=True))
        alpha = jnp.exp(m - m_new)
        p = jnp.exp(s - m_new)
        l_new = alpha * l + jnp.sum(p, axis=0, keepdims=True)
        acc_new = acc * alpha + jnp.dot(vt_ref[j], p.astype(BF16), preferred_element_type=F32)
        return m_new, l_new, acc_new

    init = (jnp.full((1, 2 * tq), NEG_BIG, F32), jnp.zeros((1, 2 * tq), F32), jnp.zeros((DA_VDIM, 2 * tq), F32))
    n_full = (i * tq) // tk
    carry = lax.fori_loop(0, n_full, lambda j, c: step(j, c, False), init)
    m, l, acc = step(n_full, carry, True)

    o = acc * (1.0 / l)
    od = o[:, :tq] - lam * o[:, tq:]
    y = od * lax.rsqrt(jnp.mean(od * od, axis=0, keepdims=True) + EPS) * g_ref[...] * (1.0 - lambda_init)
    o_ref[...] = y.T.astype(BF16)


def _attention(lam_params, subln_col, qt5, k2d, vt5, batch, seq, lambda_init):
    tq, tk = ATT_TQ, ATT_TK
    nq, nk = seq // tq, seq // tk
    body = functools.partial(_attn_body, tq=tq, tk=tk, lambda_init=lambda_init)
    return pl.pallas_call(
        body,
        grid=(batch, DA_HEADS, nq),
        in_specs=[
            pl.BlockSpec(lam_params.shape, lambda b, h, i: (0, 0)),
            pl.BlockSpec(subln_col.shape, lambda b, h, i: (0, 0)),
            pl.BlockSpec((None, None, None, DA_VDIM, tq), lambda b, h, i: (b, h, i, 0, 0)),
            pl.BlockSpec((seq, LANES), lambda b, h, i: (b, h)),
            pl.BlockSpec((None, None, nk, DA_VDIM, tk), lambda b, h, i: (b, h, 0, 0, 0)),
        ],
        out_specs=pl.BlockSpec((tq, LANES), lambda b, h, i: (b * nq + i, h)),
        out_shape=jax.ShapeDtypeStruct((batch * seq, DA_WIDTH), BF16),
        compiler_params=pltpu.CompilerParams(
            dimension_semantics=("parallel", "parallel", "arbitrary"), vmem_limit_bytes=VMEM_LIMIT),
        name="diff_attention",
    )(lam_params, subln_col, qt5, k2d, vt5)


def _out_proj_body(x_ref, oda_ref, ogm_ref, w_ref, g_ref, wr_ref, x1_ref, h2_ref, lg_ref):
    x1 = (x_ref[...]
          + jnp.dot(oda_ref[...], w_ref[0:DA_WIDTH, :], preferred_element_type=F32)
          + jnp.dot(ogm_ref[...], w_ref[DA_WIDTH:DA_WIDTH + GM_WIDTH, :], preferred_element_type=F32))
    x1_ref[...] = x1
    h2 = _rms(x1, g_ref[...])
    h2_ref[...] = h2
    lg_ref[...] = jnp.dot(h2.astype(BF16), wr_ref[...], preferred_element_type=F32)


def _out_proj(x2d, o_da, o_gm, w_out_b, moe_norm, w_router_b):
    t, d = x2d.shape
    tm = ROW_TILE
    row = lambda i: (i, 0)
    full2 = lambda i: (0, 0)
    return pl.pallas_call(
        _out_proj_body,
        grid=(t // tm,),
        in_specs=[
            pl.BlockSpec((tm, d), row),
            pl.BlockSpec((tm, DA_WIDTH), row),
            pl.BlockSpec((tm, GM_WIDTH), row),
            pl.BlockSpec(w_out_b.shape, full2),
            pl.BlockSpec((1, d), full2),
            pl.BlockSpec(w_router_b.shape, full2),
        ],
        out_specs=[pl.BlockSpec((tm, d), row), pl.BlockSpec((tm, d), row), pl.BlockSpec((tm, LANES), row)],
        out_shape=[
            jax.ShapeDtypeStruct((t, d), F32),
            jax.ShapeDtypeStruct((t, d), F32),
            jax.ShapeDtypeStruct((t, LANES), F32),
        ],
        compiler_params=pltpu.CompilerParams(dimension_semantics=("parallel",), vmem_limit_bytes=VMEM_LIMIT),
        name="out_proj",
    )(x2d, o_da, o_gm, w_out_b, moe_norm, w_router_b)


def _moe_body(blk_exp_ref, tok_ref, tok_next_ref, dst_ref, h2_hbm, wg_ref, wu_ref, wd_ref, y_hbm,
              xbuf, obuf, gsem, ssem, *, rows):
    del blk_exp_ref
    i = pl.program_id(0)
    n = pl.num_programs(0)
    slot = i % 2

    def gather_row(idx_ref, r, s):
        return pltpu.make_async_copy(h2_hbm.at[pl.ds(idx_ref[0, 0, r], 1), :], xbuf.at[s, pl.ds(r, 1), :], gsem.at[s])

    def scatter_row(r, s):
        return pltpu.make_async_copy(obuf.at[s, pl.ds(r, 1), :], y_hbm.at[pl.ds(dst_ref[0, 0, r], 1), :], ssem.at[s])

    def start_gather(idx_ref, s):
        def body(r, c):
            gather_row(idx_ref, r, s).start()
            return c
        lax.fori_loop(0, rows, body, 0, unroll=8)

    def wait_rows(make, s):
        def body(r, c):
            make(r, s).wait()
            return c
        lax.fori_loop(0, rows, body, 0, unroll=8)

    @pl.when(i == 0)
    def _():
        start_gather(tok_ref, slot)

    @pl.when(i + 1 < n)
    def _():
        start_gather(tok_next_ref, 1 - slot)

    wait_rows(lambda r, s: gather_row(tok_ref, r, s), slot)

    xb = xbuf[slot].astype(BF16)
    g = jnp.dot(xb, wg_ref[0], preferred_element_type=F32)
    u = jnp.dot(xb, wu_ref[0], preferred_element_type=F32)
    act = (g * jax.nn.sigmoid(g) * u).astype(BF16)
    y = jnp.dot(act, wd_ref[0], preferred_element_type=F32)

    @pl.when(i >= 2)
    def _():
        wait_rows(scatter_row, slot)

    obuf[slot] = y

    def sbody(r, c):
        scatter_row(r, slot).start()
        return c
    lax.fori_loop(0, rows, sbody, 0, unroll=8)

    @pl.when(i == n - 1)
    def _():
        wait_rows(scatter_row, slot)

        @pl.when(n >= 2)
        def _():
            wait_rows(scatter_row, 1 - slot)


def _moe_ffn(blk_exp, row_tok, row_dst, h2, wg_b, wu_b, wd_b, n_out_rows):
    t, d = h2.shape
    f = wg_b.shape[-1]
    rows = MOE_ROWS
    n_blocks = row_tok.shape[0]
    last = n_blocks - 1
    grid_spec = pltpu.PrefetchScalarGridSpec(
        num_scalar_prefetch=1,
        grid=(n_blocks,),
        in_specs=[
            pl.BlockSpec((1, 1, rows), lambda i, be: (i, 0, 0), memory_space=pltpu.SMEM),
            pl.BlockSpec((1, 1, rows), lambda i, be: (jnp.minimum(i + 1, last), 0, 0), memory_space=pltpu.SMEM),
            pl.BlockSpec((1, 1, rows), lambda i, be: (i, 0, 0), memory_space=pltpu.SMEM),
            pl.BlockSpec(memory_space=pl.ANY),
            pl.BlockSpec((1, d, f), lambda i, be: (be[i], 0, 0)),
            pl.BlockSpec((1, d, f), lambda i, be: (be[i], 0, 0)),
            pl.BlockSpec((1, f, d), lambda i, be: (be[i], 0, 0)),
        ],
        out_specs=pl.BlockSpec(memory_space=pl.ANY),
        scratch_shapes=[
            pltpu.VMEM((2, rows, d), F32),
            pltpu.VMEM((2, rows, d), F32),
            pltpu.SemaphoreType.DMA((2,)),
            pltpu.SemaphoreType.DMA((2,)),
        ],
    )
    return pl.pallas_call(
        functools.partial(_moe_body, rows=rows),
        grid_spec=grid_spec,
        out_shape=jax.ShapeDtypeStruct((n_out_rows, d), F32),
        compiler_params=pltpu.CompilerParams(dimension_semantics=("arbitrary",), vmem_limit_bytes=VMEM_LIMIT),
        name="moe_ffn",
    )(blk_exp, row_tok, row_tok, row_dst, h2, wg_b, wu_b, wd_b)


def _finish_body(x1_ref, y_ref, gate_ref, p_ref, pn_ref, wg_ref, bg_ref, wp_ref, fn_ref, o_ref, *, last_layer):
    d = x1_ref.shape[1]
    gt = gate_ref[...]
    x2 = x1_ref[...] + gt[:, 0:1] * y_ref[:, 0:d] + gt[:, 1:2] * y_ref[:, d:2 * d]
    hg = _rms(x2, pn_ref[...]).astype(BF16)
    gate = jax.nn.sigmoid(jnp.dot(hg, wg_ref[...], preferred_element_type=F32) + bg_ref[...])
    x3 = x2 + gate * jnp.dot(p_ref[...].astype(BF16), wp_ref[...], preferred_element_type=F32)
    o_ref[...] = _rms(x3, fn_ref[...]) if last_layer else x3


def _finish(x1, y2, gates, p2d, ple_norm, w_gate_b, b_gate, w_proj_b, final_norm, last_layer):
    t, d = x1.shape
    tm = ROW_TILE
    row = lambda i: (i, 0)
    full2 = lambda i: (0, 0)
    return pl.pallas_call(
        functools.partial(_finish_body, last_layer=last_layer),
        grid=(t // tm,),
        in_specs=[
            pl.BlockSpec((tm, d), row),
            pl.BlockSpec((tm, 2 * d), row),
            pl.BlockSpec((tm, TOP_K), row),
            pl.BlockSpec((tm, p2d.shape[1]), row),
            pl.BlockSpec((1, d), full2),
            pl.BlockSpec(w_gate_b.shape, full2),
            pl.BlockSpec((1, d), full2),
            pl.BlockSpec(w_proj_b.shape, full2),
            pl.BlockSpec((1, d), full2),
        ],
        out_specs=pl.BlockSpec((tm, d), row),
        out_shape=jax.ShapeDtypeStruct((t, d), F32),
        compiler_params=pltpu.CompilerParams(dimension_semantics=("parallel",), vmem_limit_bytes=VMEM_LIMIT),
        name="finish",
    )(x1, y2, gates, p2d, ple_norm, w_gate_b, b_gate, w_proj_b, final_norm)


def _qk_column_perm():
    perm = np.zeros((DA_WIDTH,), np.int32)
    for h in range(DA_HEADS):
        for half in range(2):
            for mp in range(2):
                for i in range(ROPE_HALF):
                    perm[h * LANES + half * 64 + mp * ROPE_HALF + i] = h * LANES + mp * DA_HEAD_DIM + half * ROPE_HALF + i
    return perm


def _rope_tables(seq):
    inv = 1.0 / (ROPE_THETA ** (jnp.arange(0, DA_HEAD_DIM, 2, dtype=F32) / DA_HEAD_DIM))
    ang = jnp.arange(seq, dtype=F32)[:, None] * inv[None, :]
    cos = jnp.tile(jnp.cos(ang), (1, LANES // ROPE_HALF))
    sin = jnp.tile(jnp.sin(ang), (1, LANES // ROPE_HALF))
    sign = jnp.where(jnp.arange(LANES) < LANES // 2, -1.0, 1.0).astype(F32)
    return cos, sin * sign[None, :]


def _route(logits, n_tokens):
    rows = MOE_ROWS
    grp = logits[:, :N_GROUPS]
    expl = logits[:, N_GROUPS:N_GROUPS + N_EXPERTS].reshape(n_tokens, N_GROUPS, EXPERTS_PER_GROUP)
    g_idx = jnp.argmax(grp, axis=-1)
    g_gate = jnp.take_along_axis(jax.nn.softmax(grp, axis=-1), g_idx[:, None], axis=1)[:, 0]
    in_grp = jnp.take_along_axis(expl, g_idx[:, None, None], axis=1)[:, 0]
    top_val, top_loc = lax.top_k(in_grp, TOP_K)
    gates = jax.nn.softmax(top_val, axis=-1) * g_gate[:, None]
    e_id = (g_idx[:, None] * EXPERTS_PER_GROUP + top_loc).astype(jnp.int32)

    n_assign = n_tokens * TOP_K
    flat_e = e_id.reshape(-1)
    order = jnp.argsort(flat_e).astype(jnp.int32)
    counts = jnp.bincount(flat_e, length=N_EXPERTS).astype(jnp.int32)
    start = jnp.cumsum(counts) - counts
    padded = ((counts + rows - 1) // rows) * rows
    pend = jnp.cumsum(padded)
    pstart = pend - padded
    n_blocks = (n_assign + N_EXPERTS * rows) // rows
    blk_exp = jnp.clip(jnp.searchsorted(pend, jnp.arange(n_blocks, dtype=jnp.int32) * rows, side="right"),
                       0, N_EXPERTS - 1).astype(jnp.int32)
    r = jnp.arange(n_blocks * rows, dtype=jnp.int32)
    e_r = blk_exp[r // rows]
    li = r - pstart[e_r]
    valid = (li >= 0) & (li < counts[e_r])
    a_r = order[jnp.clip(start[e_r] + li, 0, n_assign - 1)]
    row_tok = jnp.where(valid, a_r // TOP_K, 0)
    spare = n_assign + ((r // rows) % 2) * rows + r % rows
    row_dst = jnp.where(valid, a_r, spare)
    return (gates.astype(F32), blk_exp,
            row_tok.astype(jnp.int32).reshape(n_blocks, 1, rows), row_dst.astype(jnp.int32).reshape(n_blocks, 1, rows))


def _layer(x, p_i, i, last_layer, final_norm, attn_norm, w_in, lambda_q1, lambda_k1, lambda_q2, lambda_k2,
           diff_subln, gm_ln_gain, gm_ln_bias, gm_spatial_w, gm_spatial_b, gm_out_norm, w_out, moe_norm,
           w_group_router, w_expert_router, w_expert_gate, w_expert_up, w_expert_down, ple_norm, w_ple_gate,
           b_ple_gate, w_ple_proj):
    batch, seq, d = x.shape
    t = batch * seq
    lambda_init = 0.8 - 0.6 * math.exp(-0.3 * i)
    x2d = x.reshape(t, d)

    perm = _qk_column_perm()
    w_in_b = jnp.concatenate(
        [w_in[:, perm], w_in[:, DA_WIDTH + perm], w_in[:, 2 * DA_WIDTH:]], axis=1).astype(BF16)
    cos_t, sin_t = _rope_tables(seq)
    q, k, v, o_gm = _in_proj(x2d, attn_norm[None, :], w_in_b, cos_t, sin_t, gm_ln_gain[None, :],
                             gm_ln_bias[None, :], gm_spatial_w, gm_spatial_b.T, gm_out_norm[None, :], seq)

    nq, nk = seq // ATT_TQ, seq // ATT_TK
    qt5 = q.reshape(batch, nq, ATT_TQ, DA_HEADS, DA_VDIM).transpose(0, 3, 1, 4, 2)
    vt5 = v.reshape(batch, nk, ATT_TK, DA_HEADS, DA_VDIM).transpose(0, 3, 1, 4, 2)
    lam_params = jnp.stack([lambda_q1, lambda_k1, lambda_q2, lambda_k2]).astype(F32)
    o_da = _attention(lam_params, diff_subln[:, None], qt5, k, vt5, batch, seq, lambda_init)

    w_router = jnp.concatenate([w_group_router, w_expert_router], axis=1)
    w_router_b = jnp.pad(w_router, ((0, 0), (0, LANES - w_router.shape[1]))).astype(BF16)
    x1, h2, logits = _out_proj(x2d, o_da, o_gm, w_out.astype(BF16), moe_norm[None, :], w_router_b)

    gates, blk_exp, row_tok, row_dst = _route(logits, t)
    n_out_rows = t * TOP_K + 2 * MOE_ROWS
    y = _moe_ffn(blk_exp, row_tok, row_dst, h2, w_expert_gate.astype(BF16), w_expert_up.astype(BF16),
                 w_expert_down.astype(BF16), n_out_rows)
    y2 = y.reshape(n_out_rows // TOP_K, TOP_K * d)

    out = _finish(x1, y2, gates, p_i.reshape(t, -1), ple_norm[None, :], w_ple_gate.astype(BF16),
                  b_ple_gate[None, :], w_ple_proj.astype(BF16), final_norm[None, :], last_layer)
    return out.reshape(batch, seq, d)


def kernel(x, p, attn_norm, w_in, lambda_q1, lambda_k1, lambda_q2, lambda_k2, diff_subln, gm_ln_gain, gm_ln_bias, gm_spatial_w, gm_spatial_b, gm_out_norm, w_out, moe_norm, w_group_router, w_expert_router, w_expert_gate, w_expert_up, w_expert_down, ple_norm, w_ple_gate, b_ple_gate, w_ple_proj, final_norm):
    per_layer = (attn_norm, w_in, lambda_q1, lambda_k1, lambda_q2, lambda_k2, diff_subln, gm_ln_gain, gm_ln_bias,
                 gm_spatial_w, gm_spatial_b, gm_out_norm, w_out, moe_norm, w_group_router, w_expert_router,
                 w_expert_gate, w_expert_up, w_expert_down, ple_norm, w_ple_gate, b_ple_gate, w_ple_proj)
    depth = attn_norm.shape[0]
    for i in range(depth):
        x = _layer(x, p[i], i, i == depth - 1, final_norm, *(w[i] for w in per_layer))
    return x
```

```python
import functools
import math

import numpy as np
import jax
import jax.numpy as jnp
from jax import lax
from jax.experimental import pallas as pl
from jax.experimental.pallas import tpu as pltpu

F32 = jnp.float32
BF16 = jnp.bfloat16

DA_HEADS = 4
DA_HEAD_DIM = 64
DA_VDIM = 2 * DA_HEAD_DIM
DA_WIDTH = DA_HEADS * DA_VDIM
ROPE_THETA = 10000.0
GM_HEADS = 4
GM_HEAD_DIM = 128
GM_WIDTH = GM_HEADS * GM_HEAD_DIM
GM_CHUNK = 128
N_GROUPS = 4
EXPERTS_PER_GROUP = 8
N_EXPERTS = N_GROUPS * EXPERTS_PER_GROUP
TOP_K = 2
EPS = 1e-6

LANES = 128
ROPE_HALF = DA_HEAD_DIM // 2
NEG_BIG = -1e30

ROW_TILE = 256
ATT_TQ = 256
ATT_TK = 256
MOE_ROWS = 256
VMEM_LIMIT = 56 * 1024 * 1024


def _rms(x, g):
    return x * lax.rsqrt(jnp.mean(x * x, axis=-1, keepdims=True) + EPS) * g


def _gelu_tanh(x):
    c = math.sqrt(2.0 / math.pi)
    return x * (0.5 * (1.0 + jnp.tanh(c * (x + 0.044715 * (x * x * x)))))


def _in_proj_body(x_ref, g_ref, w_ref, cos_ref, sin_ref, lng_ref, lnb_ref, ws_ref, bst_ref, og_ref,
                  q_ref, k_ref, v_ref, ogm_ref, mix_ref):
    tm = x_ref.shape[0]
    h = _rms(x_ref[...], g_ref[...]).astype(BF16)
    cos = cos_ref[...]
    sin = sin_ref[...]

    zq = jnp.dot(h, w_ref[:, 0:DA_WIDTH], preferred_element_type=F32)
    zk = jnp.dot(h, w_ref[:, DA_WIDTH:2 * DA_WIDTH], preferred_element_type=F32)
    scale = math.log2(math.e) / math.sqrt(DA_HEAD_DIM)
    for hd in range(DA_HEADS):
        sl = slice(hd * LANES, (hd + 1) * LANES)
        qh = zq[:, sl]
        kh = zk[:, sl]
        q_ref[:, sl] = ((qh * cos + pltpu.roll(qh, LANES // 2, 1) * sin) * scale).astype(BF16)
        k_ref[:, sl] = (kh * cos + pltpu.roll(kh, LANES // 2, 1) * sin).astype(BF16)

    v_ref[...] = jnp.dot(h, w_ref[:, 2 * DA_WIDTH:3 * DA_WIDTH], preferred_element_type=F32).astype(BF16)

    a = _gelu_tanh(jnp.dot(h, w_ref[:, 3 * DA_WIDTH:3 * DA_WIDTH + 2 * GM_WIDTH], preferred_element_type=F32))
    u = a[:, :GM_WIDTH]
    vv = a[:, GM_WIDTH:]
    mu = jnp.mean(vv, axis=-1, keepdims=True)
    vc = vv - mu
    vln = vc * lax.rsqrt(jnp.mean(vc * vc, axis=-1, keepdims=True) + EPS) * lng_ref[...] + lnb_ref[...]
    vb = vln.astype(BF16)
    row = lax.broadcasted_iota(jnp.int32, (GM_CHUNK, GM_CHUNK), 0)
    col = lax.broadcasted_iota(jnp.int32, (GM_CHUNK, GM_CHUNK), 1)
    causal = row >= col
    for hd in range(GM_HEADS):
        wm = jnp.where(causal, ws_ref[hd], 0.0).astype(BF16)
        bias = bst_ref[:, hd:hd + 1]
        cs = slice(hd * GM_HEAD_DIM, (hd + 1) * GM_HEAD_DIM)
        for c in range(tm // GM_CHUNK):
            rs = slice(c * GM_CHUNK, (c + 1) * GM_CHUNK)
            mixed = jnp.dot(wm, vb[rs, cs], preferred_element_type=F32) + bias
            mix_ref[rs, cs] = u[rs, cs] * mixed
    ogm_ref[...] = _rms(mix_ref[...], og_ref[...]).astype(BF16)


def _in_proj(x2d, attn_norm, w_in_b, cos_t, sin_t, ln_g, ln_b, ws, bs_t, out_g, seq):
    t, d = x2d.shape
    tm = ROW_TILE
    n_pos = seq // tm
    row = lambda i: (i, 0)
    full2 = lambda i: (0, 0)
    return pl.pallas_call(
        _in_proj_body,
        grid=(t // tm,),
        in_specs=[
            pl.BlockSpec((tm, d), row),
            pl.BlockSpec((1, d), full2),
            pl.BlockSpec(w_in_b.shape, full2),
            pl.BlockSpec((tm, LANES), lambda i: (i % n_pos, 0)),
            pl.BlockSpec((tm, LANES), lambda i: (i % n_pos, 0)),
            pl.BlockSpec((1, GM_WIDTH), full2),
            pl.BlockSpec((1, GM_WIDTH), full2),
            pl.BlockSpec(ws.shape, lambda i: (0, 0, 0)),
            pl.BlockSpec(bs_t.shape, full2),
            pl.BlockSpec((1, GM_WIDTH), full2),
        ],
        out_specs=[
            pl.BlockSpec((tm, DA_WIDTH), row),
            pl.BlockSpec((tm, DA_WIDTH), row),
            pl.BlockSpec((tm, DA_WIDTH), row),
            pl.BlockSpec((tm, GM_WIDTH), row),
        ],
        out_shape=[
            jax.ShapeDtypeStruct((t, DA_WIDTH), BF16),
            jax.ShapeDtypeStruct((t, DA_WIDTH), BF16),
            jax.ShapeDtypeStruct((t, DA_WIDTH), BF16),
            jax.ShapeDtypeStruct((t, GM_WIDTH), BF16),
        ],
        scratch_shapes=[pltpu.VMEM((tm, GM_WIDTH), F32)],
        compiler_params=pltpu.CompilerParams(dimension_semantics=("parallel",), vmem_limit_bytes=VMEM_LIMIT),
        name="in_proj",
    )(x2d, attn_norm, w_in_b, cos_t, sin_t, ln_g, ln_b, ws, bs_t, out_g)


def _attn_body(lam_ref, g_ref, qt_ref, k_ref, vt_ref, o_ref, *scratch, tq, tk, lambda_init):
    i = pl.program_id(1)
    nh = qt_ref.shape[0]
    qst_refs, m_refs, l_refs, acc_refs, s_refs = (scratch[n * nh:(n + 1) * nh] for n in range(5))
    lp = lam_ref[...]
    lam = (jnp.exp(jnp.sum(lp[0:1] * lp[1:2], axis=-1, keepdims=True))
           - jnp.exp(jnp.sum(lp[2:3] * lp[3:4], axis=-1, keepdims=True)) + lambda_init)

    feat = lax.broadcasted_iota(jnp.int32, (DA_VDIM, tq), 0)
    is_map1 = (feat % DA_HEAD_DIM) < ROPE_HALF
    for h in range(nh):
        qt = qt_ref[h]
        zero = jnp.zeros_like(qt)
        qst_refs[h][:, 0:tq] = jnp.where(is_map1, qt, zero)
        qst_refs[h][:, tq:2 * tq] = jnp.where(is_map1, zero, qt)
        m_refs[h][...] = jnp.full(m_refs[h].shape, NEG_BIG, F32)
        l_refs[h][...] = jnp.zeros(l_refs[h].shape, F32)
        acc_refs[h][...] = jnp.zeros(acc_refs[h].shape, F32)

    def scores(h, j):
        row0 = pl.multiple_of(j * tk, tk)
        kj = k_ref[pl.ds(row0, tk), h * LANES:(h + 1) * LANES]
        return jnp.dot(kj, qst_refs[h][...], preferred_element_type=F32)

    def step(j, masked):
        for h in range(nh):
            s = s_refs[h][...]
            if masked:
                kpos = j * tk + lax.broadcasted_iota(jnp.int32, s.shape, 0)
                qpos = i * tq + lax.broadcasted_iota(jnp.int32, s.shape, 1) % tq
                s = jnp.where(kpos <= qpos, s, NEG_BIG)
            m_old = m_refs[h][...]
            m_new = jnp.maximum(m_old, jnp.max(s, axis=0, keepdims=True))
            alpha = jnp.exp2(m_old - m_new)
            p = jnp.exp2(s - m_new)
            l_refs[h][...] = alpha * l_refs[h][...] + jnp.sum(p, axis=0, keepdims=True)
            m_refs[h][...] = m_new
            if not masked:
                s_refs[h][...] = scores(h, j + 1)
            acc_refs[h][...] = (acc_refs[h][...] * alpha
                                + jnp.dot(vt_ref[h, j], p.astype(BF16), preferred_element_type=F32))

    n_full = (i * tq) // tk
    for h in range(nh):
        s_refs[h][...] = scores(h, 0)

    def full_step(j, c):
        step(j, False)
        return c
    lax.fori_loop(0, n_full, full_step, 0)
    step(n_full, True)

    for h in range(nh):
        o = acc_refs[h][...] * (1.0 / l_refs[h][...])
        od = o[:, :tq] - lam * o[:, tq:]
        y = od * lax.rsqrt(jnp.mean(od * od, axis=0, keepdims=True) + EPS) * g_ref[...] * (1.0 - lambda_init)
        o_ref[:, h * LANES:(h + 1) * LANES] = y.T.astype(BF16)


def _attention(lam_params, subln_col, qt5, k2d, vt5, batch, seq, lambda_init):
    tq, tk = ATT_TQ, ATT_TK
    nq, nk = seq // tq, seq // tk
    nh = DA_HEADS
    body = functools.partial(_attn_body, tq=tq, tk=tk, lambda_init=lambda_init)
    return pl.pallas_call(
        body,
        grid=(batch, nq),
        in_specs=[
            pl.BlockSpec(lam_params.shape, lambda b, i: (0, 0)),
            pl.BlockSpec(subln_col.shape, lambda b, i: (0, 0)),
            pl.BlockSpec((None, nh, None, DA_VDIM, tq), lambda b, i: (b, 0, i, 0, 0)),
            pl.BlockSpec((seq, DA_WIDTH), lambda b, i: (b, 0)),
            pl.BlockSpec((None, nh, nk, DA_VDIM, tk), lambda b, i: (b, 0, 0, 0, 0)),
        ],
        out_specs=pl.BlockSpec((tq, DA_WIDTH), lambda b, i: (b * nq + i, 0)),
        out_shape=jax.ShapeDtypeStruct((batch * seq, DA_WIDTH), BF16),
        scratch_shapes=(
            [pltpu.VMEM((DA_VDIM, 2 * tq), BF16)] * nh
            + [pltpu.VMEM((1, 2 * tq), F32)] * (2 * nh)
            + [pltpu.VMEM((DA_VDIM, 2 * tq), F32)] * nh
            + [pltpu.VMEM((tk, 2 * tq), F32)] * nh),
        compiler_params=pltpu.CompilerParams(
            dimension_semantics=("parallel", "arbitrary"), vmem_limit_bytes=VMEM_LIMIT),
        name="diff_attention",
    )(lam_params, subln_col, qt5, k2d, vt5)


def _out_proj_body(x_ref, oda_ref, ogm_ref, w_ref, g_ref, wr_ref, x1_ref, h2_ref, lg_ref):
    x1 = (x_ref[...]
          + jnp.dot(oda_ref[...], w_ref[0:DA_WIDTH, :], preferred_element_type=F32)
          + jnp.dot(ogm_ref[...], w_ref[DA_WIDTH:DA_WIDTH + GM_WIDTH, :], preferred_element_type=F32))
    x1_ref[...] = x1
    h2 = _rms(x1, g_ref[...])
    h2_ref[...] = h2
    lg_ref[...] = jnp.dot(h2.astype(BF16), wr_ref[...], preferred_element_type=F32)


def _out_proj(x2d, o_da, o_gm, w_out_b, moe_norm, w_router_b):
    t, d = x2d.shape
    tm = ROW_TILE
    row = lambda i: (i, 0)
    full2 = lambda i: (0, 0)
    return pl.pallas_call(
        _out_proj_body,
        grid=(t // tm,),
        in_specs=[
            pl.BlockSpec((tm, d), row),
            pl.BlockSpec((tm, DA_WIDTH), row),
            pl.BlockSpec((tm, GM_WIDTH), row),
            pl.BlockSpec(w_out_b.shape, full2),
            pl.BlockSpec((1, d), full2),
            pl.BlockSpec(w_router_b.shape, full2),
        ],
        out_specs=[pl.BlockSpec((tm, d), row), pl.BlockSpec((tm, d), row), pl.BlockSpec((tm, LANES), row)],
        out_shape=[
            jax.ShapeDtypeStruct((t, d), F32),
            jax.ShapeDtypeStruct((t, d), F32),
            jax.ShapeDtypeStruct((t, LANES), F32),
        ],
        compiler_params=pltpu.CompilerParams(dimension_semantics=("parallel",), vmem_limit_bytes=VMEM_LIMIT),
        name="out_proj",
    )(x2d, o_da, o_gm, w_out_b, moe_norm, w_router_b)


def _moe_body(blk_exp_ref, tok_ref, tok_next_ref, dst_ref, h2_hbm, wg_ref, wu_ref, wd_ref, y_hbm,
              xbuf, obuf, gsem, ssem, *, rows):
    del blk_exp_ref
    i = pl.program_id(0)
    n = pl.num_programs(0)
    slot = i % 2

    def gather_row(idx_ref, r, s):
        return pltpu.make_async_copy(h2_hbm.at[pl.ds(idx_ref[0, 0, r], 1), :], xbuf.at[s, pl.ds(r, 1), :], gsem.at[s])

    def scatter_row(r, s):
        return pltpu.make_async_copy(obuf.at[s, pl.ds(r, 1), :], y_hbm.at[pl.ds(dst_ref[0, 0, r], 1), :], ssem.at[s])

    def start_gather(idx_ref, s):
        def body(r, c):
            gather_row(idx_ref, r, s).start()
            return c
        lax.fori_loop(0, rows, body, 0, unroll=8)

    def wait_rows(make, s):
        def body(r, c):
            make(r, s).wait()
            return c
        lax.fori_loop(0, rows, body, 0, unroll=8)

    @pl.when(i == 0)
    def _():
        start_gather(tok_ref, slot)

    @pl.when(i + 1 < n)
    def _():
        start_gather(tok_next_ref, 1 - slot)

    wait_rows(lambda r, s: gather_row(tok_ref, r, s), slot)

    xb = xbuf[slot].astype(BF16)
    g = jnp.dot(xb, wg_ref[0], preferred_element_type=F32)
    u = jnp.dot(xb, wu_ref[0], preferred_element_type=F32)
    act = (g * jax.nn.sigmoid(g) * u).astype(BF16)
    y = jnp.dot(act, wd_ref[0], preferred_element_type=F32)

    @pl.when(i >= 2)
    def _():
        wait_rows(scatter_row, slot)

    obuf[slot] = y

    def sbody(r, c):
        scatter_row(r, slot).start()
        return c
    lax.fori_loop(0, rows, sbody, 0, unroll=8)

    @pl.when(i == n - 1)
    def _():
        wait_rows(scatter_row, slot)

        @pl.when(n >= 2)
        def _():
            wait_rows(scatter_row, 1 - slot)


def _moe_ffn(blk_exp, row_tok, row_dst, h2, wg_b, wu_b, wd_b, n_out_rows):
    t, d = h2.shape
    f = wg_b.shape[-1]
    rows = MOE_ROWS
    n_blocks = row_tok.shape[0]
    last = n_blocks - 1
    grid_spec = pltpu.PrefetchScalarGridSpec(
        num_scalar_prefetch=1,
        grid=(n_blocks,),
        in_specs=[
            pl.BlockSpec((1, 1, rows), lambda i, be: (i, 0, 0), memory_space=pltpu.SMEM),
            pl.BlockSpec((1, 1, rows), lambda i, be: (jnp.minimum(i + 1, last), 0, 0), memory_space=pltpu.SMEM),
            pl.BlockSpec((1, 1, rows), lambda i, be: (i, 0, 0), memory_space=pltpu.SMEM),
            pl.BlockSpec(memory_space=pl.ANY),
            pl.BlockSpec((1, d, f), lambda i, be: (be[i], 0, 0)),
            pl.BlockSpec((1, d, f), lambda i, be: (be[i], 0, 0)),
            pl.BlockSpec((1, f, d), lambda i, be: (be[i], 0, 0)),
        ],
        out_specs=pl.BlockSpec(memory_space=pl.ANY),
        scratch_shapes=[
            pltpu.VMEM((2, rows, d), F32),
            pltpu.VMEM((2, rows, d), F32),
            pltpu.SemaphoreType.DMA((2,)),
            pltpu.SemaphoreType.DMA((2,)),
        ],
    )
    return pl.pallas_call(
        functools.partial(_moe_body, rows=rows),
        grid_spec=grid_spec,
        out_shape=jax.ShapeDtypeStruct((n_out_rows, d), F32),
        compiler_params=pltpu.CompilerParams(dimension_semantics=("arbitrary",), vmem_limit_bytes=VMEM_LIMIT),
        name="moe_ffn",
    )(blk_exp, row_tok, row_tok, row_dst, h2, wg_b, wu_b, wd_b)


def _finish_body(x1_ref, y_ref, gate_ref, p_ref, pn_ref, wg_ref, bg_ref, wp_ref, fn_ref, o_ref, *, last_layer):
    d = x1_ref.shape[1]
    gt = gate_ref[...]
    x2 = x1_ref[...] + gt[:, 0:1] * y_ref[:, 0:d] + gt[:, 1:2] * y_ref[:, d:2 * d]
    hg = _rms(x2, pn_ref[...]).astype(BF16)
    gate = jax.nn.sigmoid(jnp.dot(hg, wg_ref[...], preferred_element_type=F32) + bg_ref[...])
    x3 = x2 + gate * jnp.dot(p_ref[...].astype(BF16), wp_ref[...], preferred_element_type=F32)
    o_ref[...] = _rms(x3, fn_ref[...]) if last_layer else x3


def _finish(x1, y2, gates, p2d, ple_norm, w_gate_b, b_gate, w_proj_b, final_norm, last_layer):
    t, d = x1.shape
    tm = ROW_TILE
    row = lambda i: (i, 0)
    full2 = lambda i: (0, 0)
    return pl.pallas_call(
        functools.partial(_finish_body, last_layer=last_layer),
        grid=(t // tm,),
        in_specs=[
            pl.BlockSpec((tm, d), row),
            pl.BlockSpec((tm, 2 * d), row),
            pl.BlockSpec((tm, TOP_K), row),
            pl.BlockSpec((tm, p2d.shape[1]), row),
            pl.BlockSpec((1, d), full2),
            pl.BlockSpec(w_gate_b.shape, full2),
            pl.BlockSpec((1, d), full2),
            pl.BlockSpec(w_proj_b.shape, full2),
            pl.BlockSpec((1, d), full2),
        ],
        out_specs=pl.BlockSpec((tm, d), row),
        out_shape=jax.ShapeDtypeStruct((t, d), F32),
        compiler_params=pltpu.CompilerParams(dimension_semantics=("parallel",), vmem_limit_bytes=VMEM_LIMIT),
        name="finish",
    )(x1, y2, gates, p2d, ple_norm, w_gate_b, b_gate, w_proj_b, final_norm)


def _qk_column_perm():
    perm = np.zeros((DA_WIDTH,), np.int32)
    for h in range(DA_HEADS):
        for half in range(2):
            for mp in range(2):
                for i in range(ROPE_HALF):
                    perm[h * LANES + half * 64 + mp * ROPE_HALF + i] = h * LANES + mp * DA_HEAD_DIM + half * ROPE_HALF + i
    return perm


def _rope_tables(seq):
    inv = 1.0 / (ROPE_THETA ** (jnp.arange(0, DA_HEAD_DIM, 2, dtype=F32) / DA_HEAD_DIM))
    ang = jnp.arange(seq, dtype=F32)[:, None] * inv[None, :]
    cos = jnp.tile(jnp.cos(ang), (1, LANES // ROPE_HALF))
    sin = jnp.tile(jnp.sin(ang), (1, LANES // ROPE_HALF))
    sign = jnp.where(jnp.arange(LANES) < LANES // 2, -1.0, 1.0).astype(F32)
    return cos, sin * sign[None, :]


def _route(logits, n_tokens):
    rows = MOE_ROWS
    grp = logits[:, :N_GROUPS]
    expl = logits[:, N_GROUPS:N_GROUPS + N_EXPERTS].reshape(n_tokens, N_GROUPS, EXPERTS_PER_GROUP)
    g_idx = jnp.argmax(grp, axis=-1)
    g_gate = jnp.take_along_axis(jax.nn.softmax(grp, axis=-1), g_idx[:, None], axis=1)[:, 0]
    in_grp = jnp.take_along_axis(expl, g_idx[:, None, None], axis=1)[:, 0]
    top_val, top_loc = lax.top_k(in_grp, TOP_K)
    gates = jax.nn.softmax(top_val, axis=-1) * g_gate[:, None]
    e_id = (g_idx[:, None] * EXPERTS_PER_GROUP + top_loc).astype(jnp.int32)

    n_assign = n_tokens * TOP_K
    flat_e = e_id.reshape(-1)
    order = jnp.argsort(flat_e).astype(jnp.int32)
    counts = jnp.bincount(flat_e, length=N_EXPERTS).astype(jnp.int32)
    start = jnp.cumsum(counts) - counts
    padded = ((counts + rows - 1) // rows) * rows
    pend = jnp.cumsum(padded)
    pstart = pend - padded
    n_blocks = (n_assign + N_EXPERTS * rows) // rows
    blk_exp = jnp.clip(jnp.searchsorted(pend, jnp.arange(n_blocks, dtype=jnp.int32) * rows, side="right"),
                       0, N_EXPERTS - 1).astype(jnp.int32)
    r = jnp.arange(n_blocks * rows, dtype=jnp.int32)
    e_r = blk_exp[r // rows]
    li = r - pstart[e_r]
    valid = (li >= 0) & (li < counts[e_r])
    a_r = order[jnp.clip(start[e_r] + li, 0, n_assign - 1)]
    row_tok = jnp.where(valid, a_r // TOP_K, 0)
    spare = n_assign + ((r // rows) % 2) * rows + r % rows
    row_dst = jnp.where(valid, a_r, spare)
    return (gates.astype(F32), blk_exp,
            row_tok.astype(jnp.int32).reshape(n_blocks, 1, rows), row_dst.astype(jnp.int32).reshape(n_blocks, 1, rows))


def _layer(x, p_i, i, last_layer, final_norm, attn_norm, w_in, lambda_q1, lambda_k1, lambda_q2, lambda_k2,
           diff_subln, gm_ln_gain, gm_ln_bias, gm_spatial_w, gm_spatial_b, gm_out_norm, w_out, moe_norm,
           w_group_router, w_expert_router, w_expert_gate, w_expert_up, w_expert_down, ple_norm, w_ple_gate,
           b_ple_gate, w_ple_proj):
    batch, seq, d = x.shape
    t = batch * seq
    lambda_init = 0.8 - 0.6 * math.exp(-0.3 * i)
    x2d = x.reshape(t, d)

    perm = _qk_column_perm()
    w_in_b = jnp.concatenate(
        [w_in[:, perm], w_in[:, DA_WIDTH + perm], w_in[:, 2 * DA_WIDTH:]], axis=1).astype(BF16)
    cos_t, sin_t = _rope_tables(seq)
    q, k, v, o_gm = _in_proj(x2d, attn_norm[None, :], w_in_b, cos_t, sin_t, gm_ln_gain[None, :],
                             gm_ln_bias[None, :], gm_spatial_w, gm_spatial_b.T, gm_out_norm[None, :], seq)

    nq, nk = seq // ATT_TQ, seq // ATT_TK
    qt5 = q.reshape(batch, nq, ATT_TQ, DA_HEADS, DA_VDIM).transpose(0, 3, 1, 4, 2)
    vt5 = v.reshape(batch, nk, ATT_TK, DA_HEADS, DA_VDIM).transpose(0, 3, 1, 4, 2)
    lam_params = jnp.stack([lambda_q1, lambda_k1, lambda_q2, lambda_k2]).astype(F32)
    o_da = _attention(lam_params, diff_subln[:, None], qt5, k, vt5, batch, seq, lambda_init)

    w_router = jnp.concatenate([w_group_router, w_expert_router], axis=1)
    w_router_b = jnp.pad(w_router, ((0, 0), (0, LANES - w_router.shape[1]))).astype(BF16)
    x1, h2, logits = _out_proj(x2d, o_da, o_gm, w_out.astype(BF16), moe_norm[None, :], w_router_b)

    gates, blk_exp, row_tok, row_dst = _route(logits, t)
    n_out_rows = t * TOP_K + 2 * MOE_ROWS
    y = _moe_ffn(blk_exp, row_tok, row_dst, h2, w_expert_gate.astype(BF16), w_expert_up.astype(BF16),
                 w_expert_down.astype(BF16), n_out_rows)
    y2 = y.reshape(n_out_rows // TOP_K, TOP_K * d)

    out = _finish(x1, y2, gates, p_i.reshape(t, -1), ple_norm[None, :], w_ple_gate.astype(BF16),
                  b_ple_gate[None, :], w_ple_proj.astype(BF16), final_norm[None, :], last_layer)
    return out.reshape(batch, seq, d)


def kernel(x, p, attn_norm, w_in, lambda_q1, lambda_k1, lambda_q2, lambda_k2, diff_subln, gm_ln_gain, gm_ln_bias, gm_spatial_w, gm_spatial_b, gm_out_norm, w_out, moe_norm, w_group_router, w_expert_router, w_expert_gate, w_expert_up, w_expert_down, ple_norm, w_ple_gate, b_ple_gate, w_ple_proj, final_norm):
    per_layer = (attn_norm, w_in, lambda_q1, lambda_k1, lambda_q2, lambda_k2, diff_subln, gm_ln_gain, gm_ln_bias,
                 gm_spatial_w, gm_spatial_b, gm_out_norm, w_out, moe_norm, w_group_router, w_expert_router,
                 w_expert_gate, w_expert_up, w_expert_down, ple_norm, w_ple_gate, b_ple_gate, w_ple_proj)
    depth = attn_norm.shape[0]
    for i in range(depth):
        x = _layer(x, p[i], i, i == depth - 1, final_norm, *(w[i] for w in per_layer))
    return x
```

```python
import functools
import math

import numpy as np
import jax
import jax.numpy as jnp
from jax import lax
from jax.experimental import pallas as pl
from jax.experimental.pallas import tpu as pltpu

F32 = jnp.float32
BF16 = jnp.bfloat16

DA_HEADS = 4
DA_HEAD_DIM = 64
DA_VDIM = 2 * DA_HEAD_DIM
DA_WIDTH = DA_HEADS * DA_VDIM
ROPE_THETA = 10000.0
GM_HEADS = 4
GM_HEAD_DIM = 128
GM_WIDTH = GM_HEADS * GM_HEAD_DIM
GM_CHUNK = 128
N_GROUPS = 4
EXPERTS_PER_GROUP = 8
N_EXPERTS = N_GROUPS * EXPERTS_PER_GROUP
TOP_K = 2
EPS = 1e-6

LANES = 128
ROPE_HALF = DA_HEAD_DIM // 2
NEG_BIG = -1e30

ROW_TILE = 256
ATT_TQ = 256
ATT_TK = 256
MOE_ROWS = 256
VMEM_LIMIT = 56 * 1024 * 1024


def _rms(x, g):
    return x * lax.rsqrt(jnp.mean(x * x, axis=-1, keepdims=True) + EPS) * g


def _gelu_tanh(x):
    c = math.sqrt(2.0 / math.pi)
    return x * (0.5 * (1.0 + jnp.tanh(c * (x + 0.044715 * (x * x * x)))))


def _in_proj_body(x_ref, g_ref, w_ref, cos_ref, sin_ref, lng_ref, lnb_ref, ws_ref, bst_ref, og_ref,
                  q_ref, k_ref, v_ref, ogm_ref, mix_ref):
    tm = x_ref.shape[0]
    h = _rms(x_ref[...], g_ref[...]).astype(BF16)
    cos = cos_ref[...]
    sin = sin_ref[...]

    zq = jnp.dot(h, w_ref[:, 0:DA_WIDTH], preferred_element_type=F32)
    zk = jnp.dot(h, w_ref[:, DA_WIDTH:2 * DA_WIDTH], preferred_element_type=F32)
    scale = math.log2(math.e) / math.sqrt(DA_HEAD_DIM)
    for hd in range(DA_HEADS):
        sl = slice(hd * LANES, (hd + 1) * LANES)
        qh = zq[:, sl]
        kh = zk[:, sl]
        q_ref[:, sl] = ((qh * cos + pltpu.roll(qh, LANES // 2, 1) * sin) * scale).astype(BF16)
        k_ref[:, sl] = (kh * cos + pltpu.roll(kh, LANES // 2, 1) * sin).astype(BF16)

    v_ref[...] = jnp.dot(h, w_ref[:, 2 * DA_WIDTH:3 * DA_WIDTH], preferred_element_type=F32).astype(BF16)

    a = _gelu_tanh(jnp.dot(h, w_ref[:, 3 * DA_WIDTH:3 * DA_WIDTH + 2 * GM_WIDTH], preferred_element_type=F32))
    u = a[:, :GM_WIDTH]
    vv = a[:, GM_WIDTH:]
    mu = jnp.mean(vv, axis=-1, keepdims=True)
    vc = vv - mu
    vln = vc * lax.rsqrt(jnp.mean(vc * vc, axis=-1, keepdims=True) + EPS) * lng_ref[...] + lnb_ref[...]
    vb = vln.astype(BF16)
    row = lax.broadcasted_iota(jnp.int32, (GM_CHUNK, GM_CHUNK), 0)
    col = lax.broadcasted_iota(jnp.int32, (GM_CHUNK, GM_CHUNK), 1)
    causal = row >= col
    for hd in range(GM_HEADS):
        wm = jnp.where(causal, ws_ref[hd], 0.0).astype(BF16)
        bias = bst_ref[:, hd:hd + 1]
        cs = slice(hd * GM_HEAD_DIM, (hd + 1) * GM_HEAD_DIM)
        for c in range(tm // GM_CHUNK):
            rs = slice(c * GM_CHUNK, (c + 1) * GM_CHUNK)
            mixed = jnp.dot(wm, vb[rs, cs], preferred_element_type=F32) + bias
            mix_ref[rs, cs] = u[rs, cs] * mixed
    ogm_ref[...] = _rms(mix_ref[...], og_ref[...]).astype(BF16)


def _in_proj(x2d, attn_norm, w_in_b, cos_t, sin_t, ln_g, ln_b, ws, bs_t, out_g, seq):
    t, d = x2d.shape
    tm = ROW_TILE
    n_pos = seq // tm
    row = lambda i: (i, 0)
    full2 = lambda i: (0, 0)
    return pl.pallas_call(
        _in_proj_body,
        grid=(t // tm,),
        in_specs=[
            pl.BlockSpec((tm, d), row),
            pl.BlockSpec((1, d), full2),
            pl.BlockSpec(w_in_b.shape, full2),
            pl.BlockSpec((tm, LANES), lambda i: (i % n_pos, 0)),
            pl.BlockSpec((tm, LANES), lambda i: (i % n_pos, 0)),
            pl.BlockSpec((1, GM_WIDTH), full2),
            pl.BlockSpec((1, GM_WIDTH), full2),
            pl.BlockSpec(ws.shape, lambda i: (0, 0, 0)),
            pl.BlockSpec(bs_t.shape, full2),
            pl.BlockSpec((1, GM_WIDTH), full2),
        ],
        out_specs=[
            pl.BlockSpec((tm, DA_WIDTH), row),
            pl.BlockSpec((tm, DA_WIDTH), row),
            pl.BlockSpec((tm, DA_WIDTH), row),
            pl.BlockSpec((tm, GM_WIDTH), row),
        ],
        out_shape=[
            jax.ShapeDtypeStruct((t, DA_WIDTH), BF16),
            jax.ShapeDtypeStruct((t, DA_WIDTH), BF16),
            jax.ShapeDtypeStruct((t, DA_WIDTH), BF16),
            jax.ShapeDtypeStruct((t, GM_WIDTH), BF16),
        ],
        scratch_shapes=[pltpu.VMEM((tm, GM_WIDTH), F32)],
        compiler_params=pltpu.CompilerParams(dimension_semantics=("parallel",), vmem_limit_bytes=VMEM_LIMIT),
        name="in_proj",
    )(x2d, attn_norm, w_in_b, cos_t, sin_t, ln_g, ln_b, ws, bs_t, out_g)


def _attn_body(lam_ref, g_ref, qt_ref, k_ref, vt_ref, o_ref, *scratch, tq, tk, lambda_init):
    i = pl.program_id(1)
    nh = qt_ref.shape[0]
    qst_refs, m_refs, l_refs, acc_refs, s_refs = (scratch[n * nh:(n + 1) * nh] for n in range(5))
    lp = lam_ref[...]
    lam = (jnp.exp(jnp.sum(lp[0:1] * lp[1:2], axis=-1, keepdims=True))
           - jnp.exp(jnp.sum(lp[2:3] * lp[3:4], axis=-1, keepdims=True)) + lambda_init)

    feat = lax.broadcasted_iota(jnp.int32, (DA_VDIM, tq), 0)
    is_map1 = (feat % DA_HEAD_DIM) < ROPE_HALF
    for h in range(nh):
        qt = qt_ref[h]
        zero = jnp.zeros_like(qt)
        qst_refs[h][:, 0:tq] = jnp.where(is_map1, qt, zero)
        qst_refs[h][:, tq:2 * tq] = jnp.where(is_map1, zero, qt)
        m_refs[h][...] = jnp.full(m_refs[h].shape, NEG_BIG, F32)
        l_refs[h][...] = jnp.zeros(l_refs[h].shape, F32)
        acc_refs[h][...] = jnp.zeros(acc_refs[h].shape, F32)

    def scores(h, j):
        row0 = pl.multiple_of(j * tk, tk)
        kj = k_ref[pl.ds(row0, tk), h * LANES:(h + 1) * LANES]
        return jnp.dot(kj, qst_refs[h][...], preferred_element_type=F32)

    def step(j, masked):
        for h in range(nh):
            s = s_refs[h][...]
            if masked:
                kpos = j * tk + lax.broadcasted_iota(jnp.int32, s.shape, 0)
                qpos = i * tq + lax.broadcasted_iota(jnp.int32, s.shape, 1) % tq
                s = jnp.where(kpos <= qpos, s, NEG_BIG)
            m_old = m_refs[h][...]
            m_new = jnp.maximum(m_old, jnp.max(s, axis=0, keepdims=True))
            alpha = jnp.exp2(m_old - m_new)
            p = jnp.exp2(s - m_new)
            l_refs[h][...] = alpha * l_refs[h][...] + jnp.sum(p, axis=0, keepdims=True)
            m_refs[h][...] = m_new
            if not masked:
                s_refs[h][...] = scores(h, j + 1)
            acc_refs[h][...] = (acc_refs[h][...] * alpha
                                + jnp.dot(vt_ref[h, j], p.astype(BF16), preferred_element_type=F32))

    n_full = (i * tq) // tk
    for h in range(nh):
        s_refs[h][...] = scores(h, 0)

    def full_step(j, c):
        step(j, False)
        return c
    lax.fori_loop(0, n_full, full_step, 0)
    step(n_full, True)

    for h in range(nh):
        o = acc_refs[h][...] * (1.0 / l_refs[h][...])
        od = o[:, :tq] - lam * o[:, tq:]
        y = od * lax.rsqrt(jnp.mean(od * od, axis=0, keepdims=True) + EPS) * g_ref[...] * (1.0 - lambda_init)
        o_ref[:, h * LANES:(h + 1) * LANES] = y.T.astype(BF16)


def _attention(lam_params, subln_col, qt5, k2d, vt5, batch, seq, lambda_init):
    tq, tk = ATT_TQ, ATT_TK
    nq, nk = seq // tq, seq // tk
    nh = DA_HEADS
    body = functools.partial(_attn_body, tq=tq, tk=tk, lambda_init=lambda_init)
    return pl.pallas_call(
        body,
        grid=(batch, nq),
        in_specs=[
            pl.BlockSpec(lam_params.shape, lambda b, i: (0, 0)),
            pl.BlockSpec(subln_col.shape, lambda b, i: (0, 0)),
            pl.BlockSpec((None, nh, None, DA_VDIM, tq), lambda b, i: (b, 0, i, 0, 0)),
            pl.BlockSpec((seq, DA_WIDTH), lambda b, i: (b, 0)),
            pl.BlockSpec((None, nh, nk, DA_VDIM, tk), lambda b, i: (b, 0, 0, 0, 0)),
        ],
        out_specs=pl.BlockSpec((tq, DA_WIDTH), lambda b, i: (b * nq + i, 0)),
        out_shape=jax.ShapeDtypeStruct((batch * seq, DA_WIDTH), BF16),
        scratch_shapes=(
            [pltpu.VMEM((DA_VDIM, 2 * tq), BF16)] * nh
            + [pltpu.VMEM((1, 2 * tq), F32)] * (2 * nh)
            + [pltpu.VMEM((DA_VDIM, 2 * tq), F32)] * nh
            + [pltpu.VMEM((tk, 2 * tq), F32)] * nh),
        compiler_params=pltpu.CompilerParams(
            dimension_semantics=("parallel", "arbitrary"), vmem_limit_bytes=VMEM_LIMIT),
        name="diff_attention",
    )(lam_params, subln_col, qt5, k2d, vt5)


RT_GATE = 0
RT_EXPERT = 2
RT_RANK = 4
ROUTER_EXPERT_LANE0 = N_GROUPS


TILE_ROWS = 8


def _to_row_tiles(ref, value):
    rows = value.shape[0]
    for c in range(TILE_ROWS):
        ref[pl.ds(c, rows, stride=TILE_ROWS), :] = value[:, c * LANES:(c + 1) * LANES]


def _from_row_tiles(ref, row0, rows):
    return jnp.concatenate(
        [ref[pl.ds(row0 * TILE_ROWS + c, rows, stride=TILE_ROWS), :] for c in range(TILE_ROWS)], axis=1)


def _tile_of(ref, row):
    return ref.at[pl.ds(pl.multiple_of(row * TILE_ROWS, TILE_ROWS), TILE_ROWS)]


def _out_proj_body(x_ref, oda_ref, ogm_ref, w_ref, g_ref, wr_ref, x1_ref, h2_ref, rt_ref, cnt_ref, carry_ref):
    tm = x_ref.shape[0]

    @pl.when(pl.program_id(0) == 0)
    def _():
        carry_ref[...] = jnp.zeros(carry_ref.shape, F32)

    x1 = (x_ref[...]
          + jnp.dot(oda_ref[...], w_ref[0:DA_WIDTH, :], preferred_element_type=F32)
          + jnp.dot(ogm_ref[...], w_ref[DA_WIDTH:DA_WIDTH + GM_WIDTH, :], preferred_element_type=F32))
    x1_ref[...] = x1
    h2 = _rms(x1, g_ref[...])
    _to_row_tiles(h2_ref, h2)
    lg = jnp.dot(h2.astype(BF16), wr_ref[...], preferred_element_type=F32)

    lane = lax.broadcasted_iota(jnp.int32, lg.shape, 1)
    ninf = -jnp.inf
    first_lane = lambda hit: jnp.min(jnp.where(hit, lane, LANES), axis=-1, keepdims=True)
    is_grp = lane < N_GROUPS
    gl = jnp.where(is_grp, lg, ninf)
    gmax = jnp.max(gl, axis=-1, keepdims=True)
    g_idx = first_lane(gl == gmax)
    g_gate = 1.0 / jnp.sum(jnp.where(is_grp, jnp.exp(lg - gmax), 0.0), axis=-1, keepdims=True)
    e_of_lane = lane - ROUTER_EXPERT_LANE0
    e_lo = g_idx * EXPERTS_PER_GROUP
    in_grp = (e_of_lane >= e_lo) & (e_of_lane < e_lo + EXPERTS_PER_GROUP)
    el = jnp.where(in_grp, lg, ninf)
    v1 = jnp.max(el, axis=-1, keepdims=True)
    i1 = first_lane(el == v1)
    el2 = jnp.where(lane == i1, ninf, el)
    v2 = jnp.max(el2, axis=-1, keepdims=True)
    i2 = first_lane(el2 == v2)
    t21 = jnp.exp(v2 - v1)
    w1 = g_gate / (1.0 + t21)
    w2 = g_gate * t21 / (1.0 + t21)

    hit1 = lane == i1
    hit2 = lane == i2
    chosen = (hit1 | hit2).astype(BF16)
    r_i = lax.broadcasted_iota(jnp.int32, (tm, tm), 0)
    c_i = lax.broadcasted_iota(jnp.int32, (tm, tm), 1)
    before = jnp.where(r_i > c_i, 1.0, 0.0).astype(BF16)
    base = carry_ref[...] + jnp.dot(before, chosen, preferred_element_type=F32)
    rank1 = jnp.sum(jnp.where(hit1, base, 0.0), axis=-1, keepdims=True)
    rank2 = jnp.sum(jnp.where(hit2, base, 0.0), axis=-1, keepdims=True)
    carry_ref[...] = carry_ref[...] + jnp.sum(chosen.astype(F32), axis=0, keepdims=True)
    cnt_ref[...] = carry_ref[...]

    rec = jnp.zeros(lg.shape, F32)
    for k, val in ((RT_GATE, w1), (RT_GATE + 1, w2),
                   (RT_EXPERT, (i1 - ROUTER_EXPERT_LANE0).astype(F32)),
                   (RT_EXPERT + 1, (i2 - ROUTER_EXPERT_LANE0).astype(F32)),
                   (RT_RANK, rank1), (RT_RANK + 1, rank2)):
        rec = jnp.where(lane == k, val, rec)
    rt_ref[...] = rec


def _out_proj(x2d, o_da, o_gm, w_out_b, moe_norm, w_router_b):
    t, d = x2d.shape
    tm = ROW_TILE
    row = lambda i: (i, 0)
    full2 = lambda i: (0, 0)
    return pl.pallas_call(
        _out_proj_body,
        grid=(t // tm,),
        in_specs=[
            pl.BlockSpec((tm, d), row),
            pl.BlockSpec((tm, DA_WIDTH), row),
            pl.BlockSpec((tm, GM_WIDTH), row),
            pl.BlockSpec(w_out_b.shape, full2),
            pl.BlockSpec((1, d), full2),
            pl.BlockSpec(w_router_b.shape, full2),
        ],
        out_specs=[
            pl.BlockSpec((tm, d), row),
            pl.BlockSpec((tm * TILE_ROWS, LANES), row),
            pl.BlockSpec((tm, LANES), row),
            pl.BlockSpec((1, LANES), full2),
        ],
        out_shape=[
            jax.ShapeDtypeStruct((t, d), F32),
            jax.ShapeDtypeStruct((t * TILE_ROWS, LANES), F32),
            jax.ShapeDtypeStruct((t, LANES), F32),
            jax.ShapeDtypeStruct((1, LANES), F32),
        ],
        scratch_shapes=[pltpu.VMEM((1, LANES), F32)],
        compiler_params=pltpu.CompilerParams(dimension_semantics=("arbitrary",), vmem_limit_bytes=VMEM_LIMIT),
        name="out_proj",
    )(x2d, o_da, o_gm, w_out_b, moe_norm, w_router_b)


DMA_GROUP = 8


def _wait_copies(make_copy, count):
    def body(r, c):
        make_copy(r).wait()
        return c
    lax.fori_loop(0, count, body, 0, unroll=8)


def _dispatch_body(fill_lo_ref, fill_hi_ref, n_used_ref, dst_ref, h2_hbm, buf_hbm, zblk, sem, zsem, bsem,
                   *, tm, n_blocks):
    i = pl.program_id(0)
    n = pl.num_programs(0)
    slot = i % 2
    t0 = i * tm

    def row_copy(r, dst, s):
        return pltpu.make_async_copy(_tile_of(h2_hbm, t0 + r), _tile_of(buf_hbm, dst), sem.at[s])

    def issue(g, c):
        r0 = g * DMA_GROUP
        dst = [[dst_ref[0, 0, k * tm + r0 + u] for k in range(TOP_K)] for u in range(DMA_GROUP)]
        for u in range(DMA_GROUP):
            for k in range(TOP_K):
                row_copy(r0 + u, dst[u][k], slot).start()
        return c
    lax.fori_loop(0, tm // DMA_GROUP, issue, 0)

    def wait_tile(s):
        _wait_copies(lambda r: row_copy(0, 0, s), TOP_K * tm)

    @pl.when(i > 0)
    def _():
        wait_tile(1 - slot)

    @pl.when(i == n - 1)
    def _():
        wait_tile(slot)
        zblk[...] = jnp.zeros(zblk.shape, F32)
        fill = lambda r: pltpu.make_async_copy(zblk.at[pl.ds(0, TILE_ROWS)], _tile_of(buf_hbm, r), zsem)
        blk_tiles = zblk.shape[0]
        fill_blk = lambda b: pltpu.make_async_copy(
            zblk, buf_hbm.at[pl.ds(pl.multiple_of(b * blk_tiles, blk_tiles), blk_tiles)], bsem)

        def do(copy_of, method):
            def body(r, c):
                getattr(copy_of(r), method)()
                return c
            return body
        for method in ("start", "wait"):
            for e in range(N_EXPERTS):
                lax.fori_loop(fill_lo_ref[e], fill_hi_ref[e], do(fill, method), 0)
            lax.fori_loop(n_used_ref[0], n_blocks, do(fill_blk, method), 0)


def _dispatch(fill_lo, fill_hi, n_used, dest_tiles, h2_tiles, n_rows):
    t = h2_tiles.shape[0] // TILE_ROWS
    tm = ROW_TILE
    grid_spec = pltpu.PrefetchScalarGridSpec(
        num_scalar_prefetch=3,
        grid=(t // tm,),
        in_specs=[
            pl.BlockSpec((1, 1, TOP_K * tm), lambda i, lo, hi, nu: (i, 0, 0), memory_space=pltpu.SMEM),
            pl.BlockSpec(memory_space=pl.ANY),
        ],
        out_specs=pl.BlockSpec(memory_space=pl.ANY),
        scratch_shapes=[
            pltpu.VMEM((MOE_ROWS * TILE_ROWS, LANES), F32),
            pltpu.SemaphoreType.DMA((2,)),
            pltpu.SemaphoreType.DMA,
            pltpu.SemaphoreType.DMA,
        ],
    )
    return pl.pallas_call(
        functools.partial(_dispatch_body, tm=tm, n_blocks=n_rows // MOE_ROWS),
        grid_spec=grid_spec,
        out_shape=jax.ShapeDtypeStruct((n_rows * TILE_ROWS, LANES), F32),
        compiler_params=pltpu.CompilerParams(dimension_semantics=("arbitrary",), vmem_limit_bytes=VMEM_LIMIT),
        name="dispatch",
    )(fill_lo, fill_hi, n_used, dest_tiles, h2_tiles)


def _moe_body(blk_exp_ref, x_ref, wg_ref, wu_ref, wd_ref, y_ref):
    del blk_exp_ref
    xb = _from_row_tiles(x_ref, 0, MOE_ROWS).astype(BF16)
    g = jnp.dot(xb, wg_ref[0], preferred_element_type=F32)
    u = jnp.dot(xb, wu_ref[0], preferred_element_type=F32)
    act = (g * jax.nn.sigmoid(g) * u).astype(BF16)
    _to_row_tiles(y_ref, jnp.dot(act, wd_ref[0], preferred_element_type=F32))


def _moe_ffn(blk_exp, buf, wg_b, wu_b, wd_b):
    n_rows = buf.shape[0] // TILE_ROWS
    d, f = wg_b.shape[1:]
    rows = MOE_ROWS
    row_blk = lambda i, be: (i, 0)
    w_blk = lambda i, be: (be[i], 0, 0)
    grid_spec = pltpu.PrefetchScalarGridSpec(
        num_scalar_prefetch=1,
        grid=(n_rows // rows,),
        in_specs=[
            pl.BlockSpec((rows * TILE_ROWS, LANES), row_blk),
            pl.BlockSpec((1, d, f), w_blk),
            pl.BlockSpec((1, d, f), w_blk),
            pl.BlockSpec((1, f, d), w_blk),
        ],
        out_specs=pl.BlockSpec((rows * TILE_ROWS, LANES), row_blk),
    )
    return pl.pallas_call(
        _moe_body,
        grid_spec=grid_spec,
        out_shape=jax.ShapeDtypeStruct(buf.shape, F32),
        compiler_params=pltpu.CompilerParams(dimension_semantics=("parallel",), vmem_limit_bytes=VMEM_LIMIT),
        name="moe_ffn",
    )(blk_exp, buf, wg_b, wu_b, wd_b)


def _finish_body(src_ref, src_next_ref, x1_ref, rt_ref, p_ref, pn_ref, wg_ref, bg_ref, wp_ref, fn_ref, y_hbm,
                 o_ref, ybuf, sem, *, last_layer):
    tm = x1_ref.shape[0]
    i = pl.program_id(0)
    n = pl.num_programs(0)
    slot = i % 2

    def row_copy(src, r, s):
        return pltpu.make_async_copy(_tile_of(y_hbm, src), _tile_of(ybuf.at[s], r), sem.at[s])

    def start_gather(idx_ref, s):
        def body(g, c):
            r0 = g * (TOP_K * DMA_GROUP)
            src = [idx_ref[0, 0, r0 + u] for u in range(TOP_K * DMA_GROUP)]
            for u in range(TOP_K * DMA_GROUP):
                row_copy(src[u], r0 + u, s).start()
            return c
        lax.fori_loop(0, tm // DMA_GROUP, body, 0)

    @pl.when(i == 0)
    def _():
        start_gather(src_ref, slot)

    @pl.when(i + 1 < n)
    def _():
        start_gather(src_next_ref, 1 - slot)

    _wait_copies(lambda r: row_copy(0, 0, slot), TOP_K * tm)

    rt = rt_ref[...]
    y0 = _from_row_tiles(ybuf.at[slot], 0, tm)
    y1 = _from_row_tiles(ybuf.at[slot], tm, tm)
    x2 = x1_ref[...] + rt[:, RT_GATE:RT_GATE + 1] * y0 + rt[:, RT_GATE + 1:RT_GATE + 2] * y1
    hg = _rms(x2, pn_ref[...]).astype(BF16)
    gate = jax.nn.sigmoid(jnp.dot(hg, wg_ref[...], preferred_element_type=F32) + bg_ref[...])
    x3 = x2 + gate * jnp.dot(p_ref[...].astype(BF16), wp_ref[...], preferred_element_type=F32)
    o_ref[...] = _rms(x3, fn_ref[...]) if last_layer else x3


def _finish(dest_tiles, x1, rt, p2d, ple_norm, w_gate_b, b_gate, w_proj_b, final_norm, y_tiles, last_layer):
    t, d = x1.shape
    tm = ROW_TILE
    n_tiles = t // tm
    row = lambda i: (i, 0)
    full2 = lambda i: (0, 0)
    return pl.pallas_call(
        functools.partial(_finish_body, last_layer=last_layer),
        grid=(n_tiles,),
        in_specs=[
            pl.BlockSpec((1, 1, TOP_K * tm), lambda i: (i, 0, 0), memory_space=pltpu.SMEM),
            pl.BlockSpec((1, 1, TOP_K * tm), lambda i: (jnp.minimum(i + 1, n_tiles - 1), 0, 0),
                         memory_space=pltpu.SMEM),
            pl.BlockSpec((tm, d), row),
            pl.BlockSpec((tm, LANES), row),
            pl.BlockSpec((tm, p2d.shape[1]), row),
            pl.BlockSpec((1, d), full2),
            pl.BlockSpec(w_gate_b.shape, full2),
            pl.BlockSpec((1, d), full2),
            pl.BlockSpec(w_proj_b.shape, full2),
            pl.BlockSpec((1, d), full2),
            pl.BlockSpec(memory_space=pl.ANY),
        ],
        out_specs=pl.BlockSpec((tm, d), row),
        out_shape=jax.ShapeDtypeStruct((t, d), F32),
        scratch_shapes=[
            pltpu.VMEM((2, TOP_K * tm * TILE_ROWS, LANES), F32),
            pltpu.SemaphoreType.DMA((2,)),
        ],
        compiler_params=pltpu.CompilerParams(dimension_semantics=("arbitrary",), vmem_limit_bytes=VMEM_LIMIT),
        name="finish",
    )(dest_tiles, dest_tiles, x1, rt, p2d, ple_norm, w_gate_b, b_gate, w_proj_b, final_norm, y_tiles)


def _qk_column_perm():
    perm = np.zeros((DA_WIDTH,), np.int32)
    for h in range(DA_HEADS):
        for half in range(2):
            for mp in range(2):
                for i in range(ROPE_HALF):
                    perm[h * LANES + half * 64 + mp * ROPE_HALF + i] = h * LANES + mp * DA_HEAD_DIM + half * ROPE_HALF + i
    return perm


def _rope_tables(seq):
    inv = 1.0 / (ROPE_THETA ** (jnp.arange(0, DA_HEAD_DIM, 2, dtype=F32) / DA_HEAD_DIM))
    ang = jnp.arange(seq, dtype=F32)[:, None] * inv[None, :]
    cos = jnp.tile(jnp.cos(ang), (1, LANES // ROPE_HALF))
    sin = jnp.tile(jnp.sin(ang), (1, LANES // ROPE_HALF))
    sign = jnp.where(jnp.arange(LANES) < LANES // 2, -1.0, 1.0).astype(F32)
    return cos, sin * sign[None, :]


def _dispatch_tables(rt, counts_f, n_tokens):
    rows = MOE_ROWS
    tm = ROW_TILE
    counts = counts_f[0, ROUTER_EXPERT_LANE0:ROUTER_EXPERT_LANE0 + N_EXPERTS].astype(jnp.int32)
    padded = ((counts + rows - 1) // rows) * rows
    pend = jnp.cumsum(padded)
    pstart = pend - padded
    n_blocks = (n_tokens * TOP_K + N_EXPERTS * rows) // rows
    n_used = (pend[-1:] // rows).astype(jnp.int32)
    blk_start = jnp.arange(n_blocks, dtype=jnp.int32) * rows
    blk_exp = jnp.minimum(jnp.sum(pend[None, :] <= blk_start[:, None], axis=1), N_EXPERTS - 1).astype(jnp.int32)
    expert = rt[:, RT_EXPERT:RT_EXPERT + TOP_K].astype(jnp.int32)
    rank = rt[:, RT_RANK:RT_RANK + TOP_K].astype(jnp.int32)
    first_row = jnp.sum(jnp.where(expert[..., None] == jnp.arange(N_EXPERTS), pstart, 0), axis=-1)
    dest = first_row + rank
    dest_tiles = dest.reshape(n_tokens // tm, tm, TOP_K).transpose(0, 2, 1).reshape(n_tokens // tm, 1, TOP_K * tm)
    return blk_exp, n_used, (pstart + counts).astype(jnp.int32), pend.astype(jnp.int32), dest_tiles, n_blocks * rows


def _layer(x, p_i, i, last_layer, final_norm, attn_norm, w_in, lambda_q1, lambda_k1, lambda_q2, lambda_k2,
           diff_subln, gm_ln_gain, gm_ln_bias, gm_spatial_w, gm_spatial_b, gm_out_norm, w_out, moe_norm,
           w_group_router, w_expert_router, w_expert_gate, w_expert_up, w_expert_down, ple_norm, w_ple_gate,
           b_ple_gate, w_ple_proj):
    batch, seq, d = x.shape
    t = batch * seq
    lambda_init = 0.8 - 0.6 * math.exp(-0.3 * i)
    x2d = x.reshape(t, d)

    perm = _qk_column_perm()
    w_in_b = jnp.concatenate(
        [w_in[:, perm], w_in[:, DA_WIDTH + perm], w_in[:, 2 * DA_WIDTH:]], axis=1).astype(BF16)
    cos_t, sin_t = _rope_tables(seq)
    q, k, v, o_gm = _in_proj(x2d, attn_norm[None, :], w_in_b, cos_t, sin_t, gm_ln_gain[None, :],
                             gm_ln_bias[None, :], gm_spatial_w, gm_spatial_b.T, gm_out_norm[None, :], seq)

    nq, nk = seq // ATT_TQ, seq // ATT_TK
    qt5 = q.reshape(batch, nq, ATT_TQ, DA_HEADS, DA_VDIM).transpose(0, 3, 1, 4, 2)
    vt5 = v.reshape(batch, nk, ATT_TK, DA_HEADS, DA_VDIM).transpose(0, 3, 1, 4, 2)
    lam_params = jnp.stack([lambda_q1, lambda_k1, lambda_q2, lambda_k2]).astype(F32)
    o_da = _attention(lam_params, diff_subln[:, None], qt5, k, vt5, batch, seq, lambda_init)

    w_router = jnp.concatenate([w_group_router, w_expert_router], axis=1)
    w_router_b = jnp.pad(w_router, ((0, 0), (0, LANES - w_router.shape[1]))).astype(BF16)
    x1, h2_tiles, rt, counts_f = _out_proj(x2d, o_da, o_gm, w_out.astype(BF16), moe_norm[None, :], w_router_b)

    blk_exp, n_used, fill_lo, fill_hi, dest_tiles, n_rows = _dispatch_tables(rt, counts_f, t)
    buf = _dispatch(fill_lo, fill_hi, n_used, dest_tiles, h2_tiles, n_rows)
    y_tiles = _moe_ffn(blk_exp, buf, w_expert_gate.astype(BF16), w_expert_up.astype(BF16),
                       w_expert_down.astype(BF16))

    out = _finish(dest_tiles, x1, rt, p_i.reshape(t, -1), ple_norm[None, :], w_ple_gate.astype(BF16),
                  b_ple_gate[None, :], w_ple_proj.astype(BF16), final_norm[None, :], y_tiles, last_layer)
    return out.reshape(batch, seq, d)


def kernel(x, p, attn_norm, w_in, lambda_q1, lambda_k1, lambda_q2, lambda_k2, diff_subln, gm_ln_gain, gm_ln_bias, gm_spatial_w, gm_spatial_b, gm_out_norm, w_out, moe_norm, w_group_router, w_expert_router, w_expert_gate, w_expert_up, w_expert_down, ple_norm, w_ple_gate, b_ple_gate, w_ple_proj, final_norm):
    per_layer = (attn_norm, w_in, lambda_q1, lambda_k1, lambda_q2, lambda_k2, diff_subln, gm_ln_gain, gm_ln_bias,
                 gm_spatial_w, gm_spatial_b, gm_out_norm, w_out, moe_norm, w_group_router, w_expert_router,
                 w_expert_gate, w_expert_up, w_expert_down, ple_norm, w_ple_gate, b_ple_gate, w_ple_proj)
    depth = attn_norm.shape[0]
    for i in range(depth):
        x = _layer(x, p[i], i, i == depth - 1, final_norm, *(w[i] for w in per_layer))
    return x
```

```python
import functools
import math

import numpy as np
import jax
import jax.numpy as jnp
from jax import lax
from jax.experimental import pallas as pl
from jax.experimental.pallas import tpu as pltpu

F32 = jnp.float32
BF16 = jnp.bfloat16

DA_HEADS = 4
DA_HEAD_DIM = 64
DA_VDIM = 2 * DA_HEAD_DIM
DA_WIDTH = DA_HEADS * DA_VDIM
ROPE_THETA = 10000.0
GM_HEADS = 4
GM_HEAD_DIM = 128
GM_WIDTH = GM_HEADS * GM_HEAD_DIM
GM_CHUNK = 128
N_GROUPS = 4
EXPERTS_PER_GROUP = 8
N_EXPERTS = N_GROUPS * EXPERTS_PER_GROUP
TOP_K = 2
EPS = 1e-6

LANES = 128
ROPE_HALF = DA_HEAD_DIM // 2
NEG_BIG = -1e30

ROW_TILE = 256
ATT_TQ = 256
ATT_TK = 512
V_ROWS = DA_VDIM + 16
MOE_ROWS = 256
VMEM_LIMIT = 56 * 1024 * 1024


def _rms(x, g):
    return x * lax.rsqrt(jnp.mean(x * x, axis=-1, keepdims=True) + EPS) * g


def _gelu_tanh(x):
    c = math.sqrt(2.0 / math.pi)
    return x * (0.5 * (1.0 + jnp.tanh(c * (x + 0.044715 * (x * x * x)))))


def _in_proj_body(x_ref, g_ref, w_ref, cos_ref, sin_ref, lng_ref, lnb_ref, ws_ref, bst_ref, og_ref,
                  qt_ref, k_ref, vt_ref, ogm_ref, mix_ref):
    tm = x_ref.shape[0]
    h = _rms(x_ref[...], g_ref[...]).astype(BF16)
    cos = cos_ref[...]
    sin = sin_ref[...]

    zq = jnp.dot(h, w_ref[:, 0:DA_WIDTH], preferred_element_type=F32)
    zk = jnp.dot(h, w_ref[:, DA_WIDTH:2 * DA_WIDTH], preferred_element_type=F32)
    scale = math.log2(math.e) / math.sqrt(DA_HEAD_DIM)
    for hd in range(DA_HEADS):
        sl = slice(hd * LANES, (hd + 1) * LANES)
        qh = zq[:, sl]
        kh = zk[:, sl]
        qt_ref[hd] = ((qh * cos + pltpu.roll(qh, LANES // 2, 1) * sin) * scale).T.astype(BF16)
        k_ref[:, sl] = (kh * cos + pltpu.roll(kh, LANES // 2, 1) * sin).astype(BF16)

    zv = jnp.dot(h, w_ref[:, 2 * DA_WIDTH:3 * DA_WIDTH], preferred_element_type=F32)
    for hd in range(DA_HEADS):
        vt_ref[hd, 0:DA_VDIM, :] = zv[:, hd * DA_VDIM:(hd + 1) * DA_VDIM].T.astype(BF16)
        vt_ref[hd, DA_VDIM:V_ROWS, :] = jnp.ones((V_ROWS - DA_VDIM, tm), BF16)

    a = _gelu_tanh(jnp.dot(h, w_ref[:, 3 * DA_WIDTH:3 * DA_WIDTH + 2 * GM_WIDTH], preferred_element_type=F32))
    u = a[:, :GM_WIDTH]
    vv = a[:, GM_WIDTH:]
    mu = jnp.mean(vv, axis=-1, keepdims=True)
    vc = vv - mu
    vln = vc * lax.rsqrt(jnp.mean(vc * vc, axis=-1, keepdims=True) + EPS) * lng_ref[...] + lnb_ref[...]
    vb = vln.astype(BF16)
    row = lax.broadcasted_iota(jnp.int32, (GM_CHUNK, GM_CHUNK), 0)
    col = lax.broadcasted_iota(jnp.int32, (GM_CHUNK, GM_CHUNK), 1)
    causal = row >= col
    for hd in range(GM_HEADS):
        wm = jnp.where(causal, ws_ref[hd], 0.0).astype(BF16)
        bias = bst_ref[:, hd:hd + 1]
        cs = slice(hd * GM_HEAD_DIM, (hd + 1) * GM_HEAD_DIM)
        for c in range(tm // GM_CHUNK):
            rs = slice(c * GM_CHUNK, (c + 1) * GM_CHUNK)
            mixed = jnp.dot(wm, vb[rs, cs], preferred_element_type=F32) + bias
            mix_ref[rs, cs] = u[rs, cs] * mixed
    ogm_ref[...] = _rms(mix_ref[...], og_ref[...]).astype(BF16)


def _in_proj(x2d, attn_norm, w_in_b, cos_t, sin_t, ln_g, ln_b, ws, bs_t, out_g, seq):
    t, d = x2d.shape
    tm = ROW_TILE
    assert tm == ATT_TQ and ATT_TK % tm == 0
    batch = t // seq
    n_pos = seq // tm
    tiles_per_kblk = ATT_TK // tm
    row = lambda i: (i, 0)
    full2 = lambda i: (0, 0)
    return pl.pallas_call(
        _in_proj_body,
        grid=(t // tm,),
        in_specs=[
            pl.BlockSpec((tm, d), row),
            pl.BlockSpec((1, d), full2),
            pl.BlockSpec(w_in_b.shape, full2),
            pl.BlockSpec((tm, LANES), lambda i: (i % n_pos, 0)),
            pl.BlockSpec((tm, LANES), lambda i: (i % n_pos, 0)),
            pl.BlockSpec((1, GM_WIDTH), full2),
            pl.BlockSpec((1, GM_WIDTH), full2),
            pl.BlockSpec(ws.shape, lambda i: (0, 0, 0)),
            pl.BlockSpec(bs_t.shape, full2),
            pl.BlockSpec((1, GM_WIDTH), full2),
        ],
        out_specs=[
            pl.BlockSpec((None, DA_HEADS, None, DA_VDIM, tm), lambda i: (i // n_pos, 0, i % n_pos, 0, 0)),
            pl.BlockSpec((tm, DA_WIDTH), row),
            pl.BlockSpec((None, DA_HEADS, None, V_ROWS, tm),
                         lambda i: (i // n_pos, 0, (i % n_pos) // tiles_per_kblk, 0, (i % n_pos) % tiles_per_kblk)),
            pl.BlockSpec((tm, GM_WIDTH), row),
        ],
        out_shape=[
            jax.ShapeDtypeStruct((batch, DA_HEADS, seq // ATT_TQ, DA_VDIM, ATT_TQ), BF16),
            jax.ShapeDtypeStruct((t, DA_WIDTH), BF16),
            jax.ShapeDtypeStruct((batch, DA_HEADS, seq // ATT_TK, V_ROWS, ATT_TK), BF16),
            jax.ShapeDtypeStruct((t, GM_WIDTH), BF16),
        ],
        scratch_shapes=[pltpu.VMEM((tm, GM_WIDTH), F32)],
        compiler_params=pltpu.CompilerParams(dimension_semantics=("parallel",), vmem_limit_bytes=VMEM_LIMIT),
        name="in_proj",
    )(x2d, attn_norm, w_in_b, cos_t, sin_t, ln_g, ln_b, ws, bs_t, out_g)


def _attn_body(lam_ref, g_ref, qt_ref, k_ref, vt_ref, o_ref, *scratch, tq, tk, lambda_init):
    i = pl.program_id(1)
    nh = qt_ref.shape[0]
    qst_refs, m_refs, acc_refs, s_refs = (scratch[n * nh:(n + 1) * nh] for n in range(4))
    lp = lam_ref[...]
    lam = (jnp.exp(jnp.sum(lp[0:1] * lp[1:2], axis=-1, keepdims=True))
           - jnp.exp(jnp.sum(lp[2:3] * lp[3:4], axis=-1, keepdims=True)) + lambda_init)

    feat = lax.broadcasted_iota(jnp.int32, (DA_VDIM, tq), 0)
    is_map1 = (feat % DA_HEAD_DIM) < ROPE_HALF
    for h in range(nh):
        qt = qt_ref[h]
        zero = jnp.zeros_like(qt)
        qst_refs[h][:, 0:tq] = jnp.where(is_map1, qt, zero)
        qst_refs[h][:, tq:2 * tq] = jnp.where(is_map1, zero, qt)
        m_refs[h][...] = jnp.full(m_refs[h].shape, NEG_BIG, F32)
        acc_refs[h][...] = jnp.zeros(acc_refs[h].shape, F32)

    def scores(h, j):
        row0 = pl.multiple_of(j * tk, tk)
        kj = k_ref[pl.ds(row0, tk), h * LANES:(h + 1) * LANES]
        return jnp.dot(kj, qst_refs[h][...], preferred_element_type=F32)

    def step(j, masked):
        for h in range(nh):
            s = s_refs[h][...]
            if masked:
                kpos = j * tk + lax.broadcasted_iota(jnp.int32, s.shape, 0)
                qpos = i * tq + lax.broadcasted_iota(jnp.int32, s.shape, 1) % tq
                s = jnp.where(kpos <= qpos, s, NEG_BIG)
            m_old = m_refs[h][...]
            m_new = jnp.maximum(m_old, jnp.max(s, axis=0, keepdims=True))
            alpha = jnp.exp2(m_old - m_new)
            p = jnp.exp2(s - m_new)
            m_refs[h][...] = m_new
            if not masked:
                s_refs[h][...] = scores(h, j + 1)
            acc_refs[h][...] = (acc_refs[h][...] * alpha
                                + jnp.dot(vt_ref[h, j], p.astype(BF16), preferred_element_type=F32))

    n_full = (i * tq) // tk
    for h in range(nh):
        s_refs[h][...] = scores(h, 0)

    def full_step(j, c):
        step(j, False)
        return c
    lax.fori_loop(0, n_full, full_step, 0)
    step(n_full, True)

    for h in range(nh):
        o = acc_refs[h][0:DA_VDIM, :] * (1.0 / acc_refs[h][DA_VDIM:DA_VDIM + 1, :])
        od = o[:, :tq] - lam * o[:, tq:]
        y = od * lax.rsqrt(jnp.mean(od * od, axis=0, keepdims=True) + EPS) * g_ref[...] * (1.0 - lambda_init)
        o_ref[:, h * LANES:(h + 1) * LANES] = y.T.astype(BF16)


def _attention(lam_params, subln_col, qt5, k2d, vt5, batch, seq, lambda_init):
    tq, tk = ATT_TQ, ATT_TK
    nq, nk = seq // tq, seq // tk
    nh = DA_HEADS
    body = functools.partial(_attn_body, tq=tq, tk=tk, lambda_init=lambda_init)
    return pl.pallas_call(
        body,
        grid=(batch, nq),
        in_specs=[
            pl.BlockSpec(lam_params.shape, lambda b, i: (0, 0)),
            pl.BlockSpec(subln_col.shape, lambda b, i: (0, 0)),
            pl.BlockSpec((None, nh, None, DA_VDIM, tq), lambda b, i: (b, 0, i, 0, 0)),
            pl.BlockSpec((seq, DA_WIDTH), lambda b, i: (b, 0)),
            pl.BlockSpec((None, nh, nk, V_ROWS, tk), lambda b, i: (b, 0, 0, 0, 0)),
        ],
        out_specs=pl.BlockSpec((tq, DA_WIDTH), lambda b, i: (b * nq + i, 0)),
        out_shape=jax.ShapeDtypeStruct((batch * seq, DA_WIDTH), BF16),
        scratch_shapes=(
            [pltpu.VMEM((DA_VDIM, 2 * tq), BF16)] * nh
            + [pltpu.VMEM((1, 2 * tq), F32)] * nh
            + [pltpu.VMEM((V_ROWS, 2 * tq), F32)] * nh
            + [pltpu.VMEM((tk, 2 * tq), F32)] * nh),
        compiler_params=pltpu.CompilerParams(
            dimension_semantics=("parallel", "arbitrary"), vmem_limit_bytes=VMEM_LIMIT),
        name="diff_attention",
    )(lam_params, subln_col, qt5, k2d, vt5)


RT_GATE = 0
RT_EXPERT = 2
RT_RANK = 4
ROUTER_EXPERT_LANE0 = N_GROUPS


TILE_ROWS = 8


def _to_row_tiles(ref, value):
    rows = value.shape[0]
    for c in range(TILE_ROWS):
        ref[pl.ds(c, rows, stride=TILE_ROWS), :] = value[:, c * LANES:(c + 1) * LANES]


def _from_row_tiles(ref, row0, rows):
    return jnp.concatenate(
        [ref[pl.ds(row0 * TILE_ROWS + c, rows, stride=TILE_ROWS), :] for c in range(TILE_ROWS)], axis=1)


def _tile_of(ref, row):
    return ref.at[pl.ds(pl.multiple_of(row * TILE_ROWS, TILE_ROWS), TILE_ROWS)]


def _out_proj_body(x_ref, oda_ref, ogm_ref, w_ref, g_ref, wr_ref, x1_ref, h2_ref, rt_ref, cnt_ref, carry_ref):
    tm = x_ref.shape[0]

    @pl.when(pl.program_id(0) == 0)
    def _():
        carry_ref[...] = jnp.zeros(carry_ref.shape, F32)

    x1 = (x_ref[...]
          + jnp.dot(oda_ref[...], w_ref[0:DA_WIDTH, :], preferred_element_type=F32)
          + jnp.dot(ogm_ref[...], w_ref[DA_WIDTH:DA_WIDTH + GM_WIDTH, :], preferred_element_type=F32))
    x1_ref[...] = x1
    h2 = _rms(x1, g_ref[...])
    _to_row_tiles(h2_ref, h2)
    lg = jnp.dot(h2.astype(BF16), wr_ref[...], preferred_element_type=F32)

    lane = lax.broadcasted_iota(jnp.int32, lg.shape, 1)
    ninf = -jnp.inf
    first_lane = lambda hit: jnp.min(jnp.where(hit, lane, LANES), axis=-1, keepdims=True)
    is_grp = lane < N_GROUPS
    gl = jnp.where(is_grp, lg, ninf)
    gmax = jnp.max(gl, axis=-1, keepdims=True)
    g_idx = first_lane(gl == gmax)
    g_gate = 1.0 / jnp.sum(jnp.where(is_grp, jnp.exp(lg - gmax), 0.0), axis=-1, keepdims=True)
    e_of_lane = lane - ROUTER_EXPERT_LANE0
    e_lo = g_idx * EXPERTS_PER_GROUP
    in_grp = (e_of_lane >= e_lo) & (e_of_lane < e_lo + EXPERTS_PER_GROUP)
    el = jnp.where(in_grp, lg, ninf)
    v1 = jnp.max(el, axis=-1, keepdims=True)
    i1 = first_lane(el == v1)
    el2 = jnp.where(lane == i1, ninf, el)
    v2 = jnp.max(el2, axis=-1, keepdims=True)
    i2 = first_lane(el2 == v2)
    t21 = jnp.exp(v2 - v1)
    w1 = g_gate / (1.0 + t21)
    w2 = g_gate * t21 / (1.0 + t21)

    hit1 = lane == i1
    hit2 = lane == i2
    chosen = (hit1 | hit2).astype(BF16)
    r_i = lax.broadcasted_iota(jnp.int32, (tm, tm), 0)
    c_i = lax.broadcasted_iota(jnp.int32, (tm, tm), 1)
    before = jnp.where(r_i > c_i, 1.0, 0.0).astype(BF16)
    base = carry_ref[...] + jnp.dot(before, chosen, preferred_element_type=F32)
    rank1 = jnp.sum(jnp.where(hit1, base, 0.0), axis=-1, keepdims=True)
    rank2 = jnp.sum(jnp.where(hit2, base, 0.0), axis=-1, keepdims=True)
    carry_ref[...] = carry_ref[...] + jnp.sum(chosen.astype(F32), axis=0, keepdims=True)
    cnt_ref[...] = carry_ref[...]

    rec = jnp.zeros(lg.shape, F32)
    for k, val in ((RT_GATE, w1), (RT_GATE + 1, w2),
                   (RT_EXPERT, (i1 - ROUTER_EXPERT_LANE0).astype(F32)),
                   (RT_EXPERT + 1, (i2 - ROUTER_EXPERT_LANE0).astype(F32)),
                   (RT_RANK, rank1), (RT_RANK + 1, rank2)):
        rec = jnp.where(lane == k, val, rec)
    rt_ref[...] = rec


def _out_proj(x2d, o_da, o_gm, w_out_b, moe_norm, w_router_b):
    t, d = x2d.shape
    tm = ROW_TILE
    row = lambda i: (i, 0)
    full2 = lambda i: (0, 0)
    return pl.pallas_call(
        _out_proj_body,
        grid=(t // tm,),
        in_specs=[
            pl.BlockSpec((tm, d), row),
            pl.BlockSpec((tm, DA_WIDTH), row),
            pl.BlockSpec((tm, GM_WIDTH), row),
            pl.BlockSpec(w_out_b.shape, full2),
            pl.BlockSpec((1, d), full2),
            pl.BlockSpec(w_router_b.shape, full2),
        ],
        out_specs=[
            pl.BlockSpec((tm, d), row),
            pl.BlockSpec((tm * TILE_ROWS, LANES), row),
            pl.BlockSpec((tm, LANES), row),
            pl.BlockSpec((1, LANES), full2),
        ],
        out_shape=[
            jax.ShapeDtypeStruct((t, d), F32),
            jax.ShapeDtypeStruct((t * TILE_ROWS, LANES), F32),
            jax.ShapeDtypeStruct((t, LANES), F32),
            jax.ShapeDtypeStruct((1, LANES), F32),
        ],
        scratch_shapes=[pltpu.VMEM((1, LANES), F32)],
        compiler_params=pltpu.CompilerParams(dimension_semantics=("arbitrary",), vmem_limit_bytes=VMEM_LIMIT),
        name="out_proj",
    )(x2d, o_da, o_gm, w_out_b, moe_norm, w_router_b)


DMA_GROUP = 8


def _wait_copies(make_copy, count):
    def body(r, c):
        make_copy(r).wait()
        return c
    lax.fori_loop(0, count, body, 0, unroll=8)


def _dispatch_body(fill_lo_ref, fill_hi_ref, n_used_ref, dst_ref, h2_ref, buf_hbm, stage, zblk, sem, zsem, bsem,
                   *, tm, n_blocks):
    i = pl.program_id(0)
    n = pl.num_programs(0)
    slot = i % 2
    stage[slot] = h2_ref[...]

    def row_copy(r, dst, s):
        return pltpu.make_async_copy(_tile_of(stage.at[s], r), _tile_of(buf_hbm, dst), sem.at[s])

    def issue(g, c):
        r0 = g * DMA_GROUP
        dst = [[dst_ref[0, 0, k * tm + r0 + u] for k in range(TOP_K)] for u in range(DMA_GROUP)]
        for u in range(DMA_GROUP):
            for k in range(TOP_K):
                row_copy(r0 + u, dst[u][k], slot).start()
        return c
    lax.fori_loop(0, tm // DMA_GROUP, issue, 0)

    def wait_tile(s):
        _wait_copies(lambda r: row_copy(0, 0, s), TOP_K * tm)

    @pl.when(i > 0)
    def _():
        wait_tile(1 - slot)

    @pl.when(i == n - 1)
    def _():
        wait_tile(slot)
        zblk[...] = jnp.zeros(zblk.shape, F32)
        fill = lambda r: pltpu.make_async_copy(zblk.at[pl.ds(0, TILE_ROWS)], _tile_of(buf_hbm, r), zsem)
        blk_tiles = zblk.shape[0]
        fill_blk = lambda b: pltpu.make_async_copy(
            zblk, buf_hbm.at[pl.ds(pl.multiple_of(b * blk_tiles, blk_tiles), blk_tiles)], bsem)

        def do(copy_of, method):
            def body(r, c):
                getattr(copy_of(r), method)()
                return c
            return body
        for method in ("start", "wait"):
            for e in range(N_EXPERTS):
                lax.fori_loop(fill_lo_ref[e], fill_hi_ref[e], do(fill, method), 0)
            lax.fori_loop(n_used_ref[0], n_blocks, do(fill_blk, method), 0)


def _dispatch(fill_lo, fill_hi, n_used, dest_tiles, h2_tiles, n_rows):
    t = h2_tiles.shape[0] // TILE_ROWS
    tm = ROW_TILE
    grid_spec = pltpu.PrefetchScalarGridSpec(
        num_scalar_prefetch=3,
        grid=(t // tm,),
        in_specs=[
            pl.BlockSpec((1, 1, TOP_K * tm), lambda i, lo, hi, nu: (i, 0, 0), memory_space=pltpu.SMEM),
            pl.BlockSpec((tm * TILE_ROWS, LANES), lambda i, lo, hi, nu: (i, 0)),
        ],
        out_specs=pl.BlockSpec(memory_space=pl.ANY),
        scratch_shapes=[
            pltpu.VMEM((2, tm * TILE_ROWS, LANES), F32),
            pltpu.VMEM((MOE_ROWS * TILE_ROWS, LANES), F32),
            pltpu.SemaphoreType.DMA((2,)),
            pltpu.SemaphoreType.DMA,
            pltpu.SemaphoreType.DMA,
        ],
    )
    return pl.pallas_call(
        functools.partial(_dispatch_body, tm=tm, n_blocks=n_rows // MOE_ROWS),
        grid_spec=grid_spec,
        out_shape=jax.ShapeDtypeStruct((n_rows * TILE_ROWS, LANES), F32),
        compiler_params=pltpu.CompilerParams(dimension_semantics=("arbitrary",), vmem_limit_bytes=VMEM_LIMIT),
        name="dispatch",
    )(fill_lo, fill_hi, n_used, dest_tiles, h2_tiles)


def _moe_body(blk_exp_ref, x_ref, wg_ref, wu_ref, wd_ref, y_ref, wg_b, wu_b, wd_b):
    i = pl.program_id(0)

    @pl.when((i == 0) | (blk_exp_ref[i] != blk_exp_ref[jnp.maximum(i - 1, 0)]))
    def _():
        wg_b[...] = wg_ref[0].astype(BF16)
        wu_b[...] = wu_ref[0].astype(BF16)
        wd_b[...] = wd_ref[0].astype(BF16)

    xb = _from_row_tiles(x_ref, 0, MOE_ROWS).astype(BF16)
    g = jnp.dot(xb, wg_b[...], preferred_element_type=F32)
    u = jnp.dot(xb, wu_b[...], preferred_element_type=F32)
    act = (g * jax.nn.sigmoid(g) * u).astype(BF16)
    _to_row_tiles(y_ref, jnp.dot(act, wd_b[...], preferred_element_type=F32))


def _moe_ffn(blk_exp, buf, wg, wu, wd):
    n_rows = buf.shape[0] // TILE_ROWS
    d, f = wg.shape[1:]
    rows = MOE_ROWS
    row_blk = lambda i, be: (i, 0)
    w_blk = lambda i, be: (be[i], 0, 0)
    grid_spec = pltpu.PrefetchScalarGridSpec(
        num_scalar_prefetch=1,
        grid=(n_rows // rows,),
        in_specs=[
            pl.BlockSpec((rows * TILE_ROWS, LANES), row_blk),
            pl.BlockSpec((1, d, f), w_blk),
            pl.BlockSpec((1, d, f), w_blk),
            pl.BlockSpec((1, f, d), w_blk),
        ],
        out_specs=pl.BlockSpec((rows * TILE_ROWS, LANES), row_blk),
        scratch_shapes=[pltpu.VMEM((d, f), BF16), pltpu.VMEM((d, f), BF16), pltpu.VMEM((f, d), BF16)],
    )
    return pl.pallas_call(
        _moe_body,
        grid_spec=grid_spec,
        out_shape=jax.ShapeDtypeStruct(buf.shape, F32),
        compiler_params=pltpu.CompilerParams(dimension_semantics=("arbitrary",), vmem_limit_bytes=VMEM_LIMIT),
        name="moe_ffn",
    )(blk_exp, buf, wg, wu, wd)


def _finish_body(src_ref, src_next_ref, x1_ref, rt_ref, p_ref, pn_ref, wg_ref, bg_ref, wp_ref, fn_ref, y_hbm,
                 o_ref, ybuf, sem, *, last_layer):
    tm = x1_ref.shape[0]
    i = pl.program_id(0)
    n = pl.num_programs(0)
    slot = i % 2

    def row_copy(src, r, s):
        return pltpu.make_async_copy(_tile_of(y_hbm, src), _tile_of(ybuf.at[s], r), sem.at[s])

    def start_gather(idx_ref, s):
        def body(g, c):
            r0 = g * (TOP_K * DMA_GROUP)
            src = [idx_ref[0, 0, r0 + u] for u in range(TOP_K * DMA_GROUP)]
            for u in range(TOP_K * DMA_GROUP):
                row_copy(src[u], r0 + u, s).start()
            return c
        lax.fori_loop(0, tm // DMA_GROUP, body, 0)

    @pl.when(i == 0)
    def _():
        start_gather(src_ref, slot)

    @pl.when(i + 1 < n)
    def _():
        start_gather(src_next_ref, 1 - slot)

    _wait_copies(lambda r: row_copy(0, 0, slot), TOP_K * tm)

    rt = rt_ref[...]
    y0 = _from_row_tiles(ybuf.at[slot], 0, tm)
    y1 = _from_row_tiles(ybuf.at[slot], tm, tm)
    x2 = x1_ref[...] + rt[:, RT_GATE:RT_GATE + 1] * y0 + rt[:, RT_GATE + 1:RT_GATE + 2] * y1
    hg = _rms(x2, pn_ref[...]).astype(BF16)
    gate = jax.nn.sigmoid(jnp.dot(hg, wg_ref[...], preferred_element_type=F32) + bg_ref[...])
    x3 = x2 + gate * jnp.dot(p_ref[...].astype(BF16), wp_ref[...], preferred_element_type=F32)
    o_ref[...] = _rms(x3, fn_ref[...]) if last_layer else x3


def _finish(dest_tiles, x1, rt, p2d, ple_norm, w_gate_b, b_gate, w_proj_b, final_norm, y_tiles, last_layer):
    t, d = x1.shape
    tm = ROW_TILE
    n_tiles = t // tm
    row = lambda i: (i, 0)
    full2 = lambda i: (0, 0)
    return pl.pallas_call(
        functools.partial(_finish_body, last_layer=last_layer),
        grid=(n_tiles,),
        in_specs=[
            pl.BlockSpec((1, 1, TOP_K * tm), lambda i: (i, 0, 0), memory_space=pltpu.SMEM),
            pl.BlockSpec((1, 1, TOP_K * tm), lambda i: (jnp.minimum(i + 1, n_tiles - 1), 0, 0),
                         memory_space=pltpu.SMEM),
            pl.BlockSpec((tm, d), row),
            pl.BlockSpec((tm, LANES), row),
            pl.BlockSpec((tm, p2d.shape[1]), row),
            pl.BlockSpec((1, d), full2),
            pl.BlockSpec(w_gate_b.shape, full2),
            pl.BlockSpec((1, d), full2),
            pl.BlockSpec(w_proj_b.shape, full2),
            pl.BlockSpec((1, d), full2),
            pl.BlockSpec(memory_space=pl.ANY),
        ],
        out_specs=pl.BlockSpec((tm, d), row),
        out_shape=jax.ShapeDtypeStruct((t, d), F32),
        scratch_shapes=[
            pltpu.VMEM((2, TOP_K * tm * TILE_ROWS, LANES), F32),
            pltpu.SemaphoreType.DMA((2,)),
        ],
        compiler_params=pltpu.CompilerParams(dimension_semantics=("arbitrary",), vmem_limit_bytes=VMEM_LIMIT),
        name="finish",
    )(dest_tiles, dest_tiles, x1, rt, p2d, ple_norm, w_gate_b, b_gate, w_proj_b, final_norm, y_tiles)


def _qk_column_perm():
    perm = np.zeros((DA_WIDTH,), np.int32)
    for h in range(DA_HEADS):
        for half in range(2):
            for mp in range(2):
                for i in range(ROPE_HALF):
                    perm[h * LANES + half * 64 + mp * ROPE_HALF + i] = h * LANES + mp * DA_HEAD_DIM + half * ROPE_HALF + i
    return perm


def _rope_tables(seq):
    inv = 1.0 / (ROPE_THETA ** (jnp.arange(0, DA_HEAD_DIM, 2, dtype=F32) / DA_HEAD_DIM))
    ang = jnp.arange(seq, dtype=F32)[:, None] * inv[None, :]
    cos = jnp.tile(jnp.cos(ang), (1, LANES // ROPE_HALF))
    sin = jnp.tile(jnp.sin(ang), (1, LANES // ROPE_HALF))
    sign = jnp.where(jnp.arange(LANES) < LANES // 2, -1.0, 1.0).astype(F32)
    return cos, sin * sign[None, :]


def _dispatch_tables(rt, counts_f, n_tokens):
    rows = MOE_ROWS
    tm = ROW_TILE
    counts = counts_f[0, ROUTER_EXPERT_LANE0:ROUTER_EXPERT_LANE0 + N_EXPERTS].astype(jnp.int32)
    padded = ((counts + rows - 1) // rows) * rows
    pend = jnp.cumsum(padded)
    pstart = pend - padded
    n_blocks = (n_tokens * TOP_K + N_EXPERTS * rows) // rows
    n_used = (pend[-1:] // rows).astype(jnp.int32)
    blk_start = jnp.arange(n_blocks, dtype=jnp.int32) * rows
    blk_exp = jnp.minimum(jnp.sum(pend[None, :] <= blk_start[:, None], axis=1), N_EXPERTS - 1).astype(jnp.int32)
    expert = rt[:, RT_EXPERT:RT_EXPERT + TOP_K].astype(jnp.int32)
    rank = rt[:, RT_RANK:RT_RANK + TOP_K].astype(jnp.int32)
    first_row = jnp.sum(jnp.where(expert[..., None] == jnp.arange(N_EXPERTS), pstart, 0), axis=-1)
    dest = first_row + rank
    dest_tiles = dest.reshape(n_tokens // tm, tm, TOP_K).transpose(0, 2, 1).reshape(n_tokens // tm, 1, TOP_K * tm)
    return blk_exp, n_used, (pstart + counts).astype(jnp.int32), pend.astype(jnp.int32), dest_tiles, n_blocks * rows


def _layer(x, p_i, i, last_layer, final_norm, attn_norm, w_in, lambda_q1, lambda_k1, lambda_q2, lambda_k2,
           diff_subln, gm_ln_gain, gm_ln_bias, gm_spatial_w, gm_spatial_b, gm_out_norm, w_out, moe_norm,
           w_group_router, w_expert_router, w_expert_gate, w_expert_up, w_expert_down, ple_norm, w_ple_gate,
           b_ple_gate, w_ple_proj):
    batch, seq, d = x.shape
    t = batch * seq
    lambda_init = 0.8 - 0.6 * math.exp(-0.3 * i)
    x2d = x.reshape(t, d)

    perm = _qk_column_perm()
    w_in_b = jnp.concatenate(
        [w_in[:, perm], w_in[:, DA_WIDTH + perm], w_in[:, 2 * DA_WIDTH:]], axis=1).astype(BF16)
    cos_t, sin_t = _rope_tables(seq)
    qt5, k, vt5, o_gm = _in_proj(x2d, attn_norm[None, :], w_in_b, cos_t, sin_t, gm_ln_gain[None, :],
                                 gm_ln_bias[None, :], gm_spatial_w, gm_spatial_b.T, gm_out_norm[None, :], seq)
    lam_params = jnp.stack([lambda_q1, lambda_k1, lambda_q2, lambda_k2]).astype(F32)
    o_da = _attention(lam_params, diff_subln[:, None], qt5, k, vt5, batch, seq, lambda_init)

    w_router = jnp.concatenate([w_group_router, w_expert_router], axis=1)
    w_router_b = jnp.pad(w_router, ((0, 0), (0, LANES - w_router.shape[1]))).astype(BF16)
    x1, h2_tiles, rt, counts_f = _out_proj(x2d, o_da, o_gm, w_out.astype(BF16), moe_norm[None, :], w_router_b)

    blk_exp, n_used, fill_lo, fill_hi, dest_tiles, n_rows = _dispatch_tables(rt, counts_f, t)
    buf = _dispatch(fill_lo, fill_hi, n_used, dest_tiles, h2_tiles, n_rows)
    y_tiles = _moe_ffn(blk_exp, buf, w_expert_gate, w_expert_up, w_expert_down)

    out = _finish(dest_tiles, x1, rt, p_i.reshape(t, -1), ple_norm[None, :], w_ple_gate.astype(BF16),
                  b_ple_gate[None, :], w_ple_proj.astype(BF16), final_norm[None, :], y_tiles, last_layer)
    return out.reshape(batch, seq, d)


def kernel(x, p, attn_norm, w_in, lambda_q1, lambda_k1, lambda_q2, lambda_k2, diff_subln, gm_ln_gain, gm_ln_bias, gm_spatial_w, gm_spatial_b, gm_out_norm, w_out, moe_norm, w_group_router, w_expert_router, w_expert_gate, w_expert_up, w_expert_down, ple_norm, w_ple_gate, b_ple_gate, w_ple_proj, final_norm):
    per_layer = (attn_norm, w_in, lambda_q1, lambda_k1, lambda_q2, lambda_k2, diff_subln, gm_ln_gain, gm_ln_bias,
                 gm_spatial_w, gm_spatial_b, gm_out_norm, w_out, moe_norm, w_group_router, w_expert_router,
                 w_expert_gate, w_expert_up, w_expert_down, ple_norm, w_ple_gate, b_ple_gate, w_ple_proj)
    depth = attn_norm.shape[0]
    for i in range(depth):
        x = _layer(x, p[i], i, i == depth - 1, final_norm, *(w[i] for w in per_layer))
    return x
```

```python
import functools
import math

import numpy as np
import jax
import jax.numpy as jnp
from jax import lax
from jax.experimental import pallas as pl
from jax.experimental.pallas import tpu as pltpu

F32 = jnp.float32
BF16 = jnp.bfloat16

DA_HEADS = 4
DA_HEAD_DIM = 64
DA_VDIM = 2 * DA_HEAD_DIM
DA_WIDTH = DA_HEADS * DA_VDIM
ROPE_THETA = 10000.0
GM_HEADS = 4
GM_HEAD_DIM = 128
GM_WIDTH = GM_HEADS * GM_HEAD_DIM
GM_CHUNK = 128
N_GROUPS = 4
EXPERTS_PER_GROUP = 8
N_EXPERTS = N_GROUPS * EXPERTS_PER_GROUP
TOP_K = 2
EPS = 1e-6

LANES = 128
ROPE_HALF = DA_HEAD_DIM // 2
NEG_BIG = -1e30

ROW_TILE = 256
ATT_TQ = 256
ATT_TK = 512
V_ROWS = DA_VDIM + 16
MOE_ROWS = 256
VMEM_LIMIT = 56 * 1024 * 1024


def _rms(x, g):
    return x * lax.rsqrt(jnp.mean(x * x, axis=-1, keepdims=True) + EPS) * g


def _gelu_tanh(x):
    c = math.sqrt(2.0 / math.pi)
    return x * (0.5 * (1.0 + jnp.tanh(c * (x + 0.044715 * (x * x * x)))))


def _in_proj_body(x_ref, g_ref, w_ref, cos_ref, sin_ref, lng_ref, lnb_ref, ws_ref, bst_ref, og_ref,
                  qt_ref, k_ref, vt_ref, ogm_ref, mix_ref):
    tm = x_ref.shape[0]
    h = _rms(x_ref[...], g_ref[...]).astype(BF16)
    cos = cos_ref[...]
    sin = sin_ref[...]

    zq = jnp.dot(h, w_ref[:, 0:DA_WIDTH], preferred_element_type=F32)
    zk = jnp.dot(h, w_ref[:, DA_WIDTH:2 * DA_WIDTH], preferred_element_type=F32)
    scale = math.log2(math.e) / math.sqrt(DA_HEAD_DIM)
    for hd in range(DA_HEADS):
        sl = slice(hd * LANES, (hd + 1) * LANES)
        qh = zq[:, sl]
        kh = zk[:, sl]
        qt_ref[hd] = ((qh * cos + pltpu.roll(qh, LANES // 2, 1) * sin) * scale).T.astype(BF16)
        k_ref[:, sl] = (kh * cos + pltpu.roll(kh, LANES // 2, 1) * sin).astype(BF16)

    zv = jnp.dot(h, w_ref[:, 2 * DA_WIDTH:3 * DA_WIDTH], preferred_element_type=F32)
    for hd in range(DA_HEADS):
        vt_ref[hd, 0:DA_VDIM, :] = zv[:, hd * DA_VDIM:(hd + 1) * DA_VDIM].T.astype(BF16)
        vt_ref[hd, DA_VDIM:V_ROWS, :] = jnp.ones((V_ROWS - DA_VDIM, tm), BF16)

    a = _gelu_tanh(jnp.dot(h, w_ref[:, 3 * DA_WIDTH:3 * DA_WIDTH + 2 * GM_WIDTH], preferred_element_type=F32))
    u = a[:, :GM_WIDTH]
    vv = a[:, GM_WIDTH:]
    mu = jnp.mean(vv, axis=-1, keepdims=True)
    vc = vv - mu
    vln = vc * lax.rsqrt(jnp.mean(vc * vc, axis=-1, keepdims=True) + EPS) * lng_ref[...] + lnb_ref[...]
    vb = vln.astype(BF16)
    row = lax.broadcasted_iota(jnp.int32, (GM_CHUNK, GM_CHUNK), 0)
    col = lax.broadcasted_iota(jnp.int32, (GM_CHUNK, GM_CHUNK), 1)
    causal = row >= col
    for hd in range(GM_HEADS):
        wm = jnp.where(causal, ws_ref[hd], 0.0).astype(BF16)
        bias = bst_ref[:, hd:hd + 1]
        cs = slice(hd * GM_HEAD_DIM, (hd + 1) * GM_HEAD_DIM)
        for c in range(tm // GM_CHUNK):
            rs = slice(c * GM_CHUNK, (c + 1) * GM_CHUNK)
            mixed = jnp.dot(wm, vb[rs, cs], preferred_element_type=F32) + bias
            mix_ref[rs, cs] = u[rs, cs] * mixed
    ogm_ref[...] = _rms(mix_ref[...], og_ref[...]).astype(BF16)


def _in_proj(x2d, attn_norm, w_in_b, cos_t, sin_t, ln_g, ln_b, ws, bs_t, out_g, seq):
    t, d = x2d.shape
    tm = ROW_TILE
    assert tm == ATT_TQ and ATT_TK % tm == 0
    batch = t // seq
    n_pos = seq // tm
    tiles_per_kblk = ATT_TK // tm
    row = lambda i: (i, 0)
    full2 = lambda i: (0, 0)
    return pl.pallas_call(
        _in_proj_body,
        grid=(t // tm,),
        in_specs=[
            pl.BlockSpec((tm, d), row),
            pl.BlockSpec((1, d), full2),
            pl.BlockSpec(w_in_b.shape, full2),
            pl.BlockSpec((tm, LANES), lambda i: (i % n_pos, 0)),
            pl.BlockSpec((tm, LANES), lambda i: (i % n_pos, 0)),
            pl.BlockSpec((1, GM_WIDTH), full2),
            pl.BlockSpec((1, GM_WIDTH), full2),
            pl.BlockSpec(ws.shape, lambda i: (0, 0, 0)),
            pl.BlockSpec(bs_t.shape, full2),
            pl.BlockSpec((1, GM_WIDTH), full2),
        ],
        out_specs=[
            pl.BlockSpec((None, DA_HEADS, None, DA_VDIM, tm), lambda i: (i // n_pos, 0, i % n_pos, 0, 0)),
            pl.BlockSpec((tm, DA_WIDTH), row),
            pl.BlockSpec((None, DA_HEADS, None, V_ROWS, tm),
                         lambda i: (i // n_pos, 0, (i % n_pos) // tiles_per_kblk, 0, (i % n_pos) % tiles_per_kblk)),
            pl.BlockSpec((tm, GM_WIDTH), row),
        ],
        out_shape=[
            jax.ShapeDtypeStruct((batch, DA_HEADS, seq // ATT_TQ, DA_VDIM, ATT_TQ), BF16),
            jax.ShapeDtypeStruct((t, DA_WIDTH), BF16),
            jax.ShapeDtypeStruct((batch, DA_HEADS, seq // ATT_TK, V_ROWS, ATT_TK), BF16),
            jax.ShapeDtypeStruct((t, GM_WIDTH), BF16),
        ],
        scratch_shapes=[pltpu.VMEM((tm, GM_WIDTH), F32)],
        compiler_params=pltpu.CompilerParams(dimension_semantics=("parallel",), vmem_limit_bytes=VMEM_LIMIT),
        name="in_proj",
    )(x2d, attn_norm, w_in_b, cos_t, sin_t, ln_g, ln_b, ws, bs_t, out_g)


def _attn_body(lam_ref, g_ref, qt_ref, qtn_ref, k_ref, vt_ref, o_ref, *scratch, tq, tk, lambda_init):
    i = pl.program_id(1)
    nh = qt_ref.shape[0]
    qst_refs, qsn_refs, m_refs, acc_refs, s_refs, bmax_refs = (scratch[n * nh:(n + 1) * nh] for n in range(6))
    cap_ref = scratch[6 * nh]
    caps = cap_ref.shape[0]
    lp = lam_ref[...]
    lam = (jnp.exp(jnp.sum(lp[0:1] * lp[1:2], axis=-1, keepdims=True))
           - jnp.exp(jnp.sum(lp[2:3] * lp[3:4], axis=-1, keepdims=True)) + lambda_init)

    feat = lax.broadcasted_iota(jnp.int32, (DA_VDIM, tq), 0)
    is_map1 = (feat % DA_HEAD_DIM) < ROPE_HALF
    for h in range(nh):
        for src, dst in ((qt_ref, qst_refs), (qtn_ref, qsn_refs)):
            qt = src[h]
            zero = jnp.zeros_like(qt)
            dst[h][:, 0:tq] = jnp.where(is_map1, qt, zero)
            dst[h][:, tq:2 * tq] = jnp.where(is_map1, zero, qt)
        m_refs[h][...] = jnp.full(m_refs[h].shape, NEG_BIG, F32)
        acc_refs[h][...] = jnp.zeros(acc_refs[h].shape, F32)

    def produce_scores(h, q_refs, j):
        row0 = pl.multiple_of(j * tk, tk)
        kj = k_ref[pl.ds(row0, tk), h * LANES:(h + 1) * LANES]
        sc = jnp.dot(kj, q_refs[h][...], preferred_element_type=F32)
        s_refs[h][...] = sc
        bmax_refs[h][...] = jnp.max(sc, axis=0, keepdims=True)

    def step(j, masked):
        for h in range(nh):
            s = s_refs[h][...]
            if masked:
                s = jnp.minimum(s, cap_ref[i % caps])
                bmax = jnp.max(s, axis=0, keepdims=True)
            else:
                bmax = bmax_refs[h][...]
            m_old = m_refs[h][...]
            m_new = jnp.maximum(m_old, bmax)
            alpha = jnp.exp2(m_old - m_new)
            p = jnp.exp2(s - m_new)
            m_refs[h][...] = m_new
            if masked:
                produce_scores(h, qsn_refs, 0)
            else:
                produce_scores(h, qst_refs, j + 1)
            acc_refs[h][...] = (acc_refs[h][...] * alpha
                                + jnp.dot(vt_ref[h, j], p.astype(BF16), preferred_element_type=F32))

    n_full = (i * tq) // tk

    @pl.when(i == 0)
    def _():
        for h in range(nh):
            produce_scores(h, qst_refs, 0)
        r_i = lax.broadcasted_iota(jnp.int32, (tk, 2 * tq), 0)
        c_i = lax.broadcasted_iota(jnp.int32, (tk, 2 * tq), 1) % tq
        for par in range(caps):
            cap_ref[par] = jnp.where(r_i - c_i <= par * tq, -NEG_BIG, NEG_BIG)

    def full_step(j, c):
        step(j, False)
        return c
    lax.fori_loop(0, n_full, full_step, 0)
    step(n_full, True)

    for h in range(nh):
        o = acc_refs[h][0:DA_VDIM, :] * (1.0 / acc_refs[h][DA_VDIM:DA_VDIM + 1, :])
        od = o[:, :tq] - lam * o[:, tq:]
        y = od * lax.rsqrt(jnp.mean(od * od, axis=0, keepdims=True) + EPS) * g_ref[...] * (1.0 - lambda_init)
        o_ref[:, h * LANES:(h + 1) * LANES] = y.T.astype(BF16)


def _attention(lam_params, subln_col, qt5, k2d, vt5, batch, seq, lambda_init):
    tq, tk = ATT_TQ, ATT_TK
    nq, nk = seq // tq, seq // tk
    nh = DA_HEADS
    body = functools.partial(_attn_body, tq=tq, tk=tk, lambda_init=lambda_init)

    def next_q(b, i):
        f = jnp.minimum(b * nq + i + 1, batch * nq - 1)
        return (f // nq, 0, f % nq, 0, 0)
    return pl.pallas_call(
        body,
        grid=(batch, nq),
        in_specs=[
            pl.BlockSpec(lam_params.shape, lambda b, i: (0, 0)),
            pl.BlockSpec(subln_col.shape, lambda b, i: (0, 0)),
            pl.BlockSpec((None, nh, None, DA_VDIM, tq), lambda b, i: (b, 0, i, 0, 0)),
            pl.BlockSpec((None, nh, None, DA_VDIM, tq), next_q),
            pl.BlockSpec((seq, DA_WIDTH), lambda b, i: (b, 0)),
            pl.BlockSpec((None, nh, nk, V_ROWS, tk), lambda b, i: (b, 0, 0, 0, 0)),
        ],
        out_specs=pl.BlockSpec((tq, DA_WIDTH), lambda b, i: (b * nq + i, 0)),
        out_shape=jax.ShapeDtypeStruct((batch * seq, DA_WIDTH), BF16),
        scratch_shapes=(
            [pltpu.VMEM((DA_VDIM, 2 * tq), BF16)] * (2 * nh)
            + [pltpu.VMEM((1, 2 * tq), F32)] * nh
            + [pltpu.VMEM((V_ROWS, 2 * tq), F32)] * nh
            + [pltpu.VMEM((tk, 2 * tq), F32)] * nh
            + [pltpu.VMEM((1, 2 * tq), F32)] * nh
            + [pltpu.VMEM((tk // tq, tk, 2 * tq), F32)]),
        compiler_params=pltpu.CompilerParams(
            dimension_semantics=("arbitrary", "arbitrary"), vmem_limit_bytes=VMEM_LIMIT),
        name="diff_attention",
    )(lam_params, subln_col, qt5, qt5, k2d, vt5)


RT_GATE = 0
RT_EXPERT = 2
RT_RANK = 4
ROUTER_EXPERT_LANE0 = N_GROUPS


TILE_ROWS = 8


def _to_row_tiles(ref, value):
    rows = value.shape[0]
    for c in range(TILE_ROWS):
        ref[pl.ds(c, rows, stride=TILE_ROWS), :] = value[:, c * LANES:(c + 1) * LANES]


def _from_row_tiles(ref, row0, rows):
    return jnp.concatenate(
        [ref[pl.ds(row0 * TILE_ROWS + c, rows, stride=TILE_ROWS), :] for c in range(TILE_ROWS)], axis=1)


def _tile_of(ref, row):
    return ref.at[pl.ds(pl.multiple_of(row * TILE_ROWS, TILE_ROWS), TILE_ROWS)]


def _out_proj_body(x_ref, oda_ref, ogm_ref, w_ref, g_ref, wr_ref, x1_ref, h2_ref, rt_ref, cnt_ref, carry_ref):
    tm = x_ref.shape[0]

    @pl.when(pl.program_id(0) == 0)
    def _():
        carry_ref[...] = jnp.zeros(carry_ref.shape, F32)

    x1 = (x_ref[...]
          + jnp.dot(oda_ref[...], w_ref[0:DA_WIDTH, :], preferred_element_type=F32)
          + jnp.dot(ogm_ref[...], w_ref[DA_WIDTH:DA_WIDTH + GM_WIDTH, :], preferred_element_type=F32))
    x1_ref[...] = x1
    h2 = _rms(x1, g_ref[...])
    _to_row_tiles(h2_ref, h2)
    lg = jnp.dot(h2.astype(BF16), wr_ref[...], preferred_element_type=F32)

    lane = lax.broadcasted_iota(jnp.int32, lg.shape, 1).astype(F32)
    ninf = -jnp.inf
    first_lane = lambda hit: jnp.min(jnp.where(hit, lane, float(LANES)), axis=-1, keepdims=True)
    is_grp = lane < N_GROUPS
    gl = jnp.where(is_grp, lg, ninf)
    gmax = jnp.max(gl, axis=-1, keepdims=True)
    g_idx = first_lane(gl == gmax)
    g_gate = 1.0 / jnp.sum(jnp.where(is_grp, jnp.exp(lg - gmax), 0.0), axis=-1, keepdims=True)
    e_of_lane = lane - ROUTER_EXPERT_LANE0
    e_lo = g_idx * EXPERTS_PER_GROUP
    in_grp = (e_of_lane >= e_lo) & (e_of_lane < e_lo + EXPERTS_PER_GROUP)
    el = jnp.where(in_grp, lg, ninf)
    v1 = jnp.max(el, axis=-1, keepdims=True)
    i1 = first_lane(el == v1)
    el2 = jnp.where(lane == i1, ninf, el)
    v2 = jnp.max(el2, axis=-1, keepdims=True)
    i2 = first_lane(el2 == v2)
    t21 = jnp.exp(v2 - v1)
    w1 = g_gate / (1.0 + t21)
    w2 = g_gate * t21 / (1.0 + t21)

    hit1 = lane == i1
    hit2 = lane == i2
    chosen = (hit1 | hit2).astype(BF16)
    r_i = lax.broadcasted_iota(jnp.int32, (tm, tm), 0)
    c_i = lax.broadcasted_iota(jnp.int32, (tm, tm), 1)
    before = jnp.where(r_i > c_i, 1.0, 0.0).astype(BF16)
    base = carry_ref[...] + jnp.dot(before, chosen, preferred_element_type=F32)
    rank1 = jnp.sum(jnp.where(hit1, base, 0.0), axis=-1, keepdims=True)
    rank2 = jnp.sum(jnp.where(hit2, base, 0.0), axis=-1, keepdims=True)
    carry_ref[...] = carry_ref[...] + jnp.sum(chosen.astype(F32), axis=0, keepdims=True)
    cnt_ref[...] = carry_ref[...]

    rec = jnp.zeros(lg.shape, F32)
    for k, val in ((RT_GATE, w1), (RT_GATE + 1, w2),
                   (RT_EXPERT, (i1 - ROUTER_EXPERT_LANE0).astype(F32)),
                   (RT_EXPERT + 1, (i2 - ROUTER_EXPERT_LANE0).astype(F32)),
                   (RT_RANK, rank1), (RT_RANK + 1, rank2)):
        rec = jnp.where(lane == k, val, rec)
    rt_ref[...] = rec


def _out_proj(x2d, o_da, o_gm, w_out_b, moe_norm, w_router_b):
    t, d = x2d.shape
    tm = ROW_TILE
    row = lambda i: (i, 0)
    full2 = lambda i: (0, 0)
    return pl.pallas_call(
        _out_proj_body,
        grid=(t // tm,),
        in_specs=[
            pl.BlockSpec((tm, d), row),
            pl.BlockSpec((tm, DA_WIDTH), row),
            pl.BlockSpec((tm, GM_WIDTH), row),
            pl.BlockSpec(w_out_b.shape, full2),
            pl.BlockSpec((1, d), full2),
            pl.BlockSpec(w_router_b.shape, full2),
        ],
        out_specs=[
            pl.BlockSpec((tm, d), row),
            pl.BlockSpec((tm * TILE_ROWS, LANES), row),
            pl.BlockSpec((tm, LANES), row),
            pl.BlockSpec((1, LANES), full2),
        ],
        out_shape=[
            jax.ShapeDtypeStruct((t, d), F32),
            jax.ShapeDtypeStruct((t * TILE_ROWS, LANES), F32),
            jax.ShapeDtypeStruct((t, LANES), F32),
            jax.ShapeDtypeStruct((1, LANES), F32),
        ],
        scratch_shapes=[pltpu.VMEM((1, LANES), F32)],
        compiler_params=pltpu.CompilerParams(dimension_semantics=("arbitrary",), vmem_limit_bytes=VMEM_LIMIT),
        name="out_proj",
    )(x2d, o_da, o_gm, w_out_b, moe_norm, w_router_b)


DMA_GROUP = 8


def _wait_copies(make_copy, count):
    def body(r, c):
        make_copy(r).wait()
        return c
    lax.fori_loop(0, count, body, 0, unroll=8)


def _dispatch_body(fill_lo_ref, fill_hi_ref, n_used_ref, dst_ref, h2_ref, buf_hbm, stage, zblk, sem, zsem, bsem,
                   *, tm, n_blocks):
    i = pl.program_id(0)
    n = pl.num_programs(0)
    slot = i % 2
    stage[slot] = h2_ref[...]

    def row_copy(r, dst, s):
        return pltpu.make_async_copy(_tile_of(stage.at[s], r), _tile_of(buf_hbm, dst), sem.at[s])

    def issue(g, c):
        r0 = g * DMA_GROUP
        dst = [[dst_ref[0, 0, k * tm + r0 + u] for k in range(TOP_K)] for u in range(DMA_GROUP)]
        for u in range(DMA_GROUP):
            for k in range(TOP_K):
                row_copy(r0 + u, dst[u][k], slot).start(priority=k % 2)
        return c
    lax.fori_loop(0, tm // DMA_GROUP, issue, 0)

    def wait_tile(s):
        _wait_copies(lambda r: row_copy(0, 0, s), TOP_K * tm)

    @pl.when(i > 0)
    def _():
        wait_tile(1 - slot)

    @pl.when(i == n - 1)
    def _():
        wait_tile(slot)
        zblk[...] = jnp.zeros(zblk.shape, F32)
        fill = lambda r: pltpu.make_async_copy(zblk.at[pl.ds(0, TILE_ROWS)], _tile_of(buf_hbm, r), zsem)
        blk_tiles = zblk.shape[0]
        fill_blk = lambda b: pltpu.make_async_copy(
            zblk, buf_hbm.at[pl.ds(pl.multiple_of(b * blk_tiles, blk_tiles), blk_tiles)], bsem)

        def do(copy_of, method):
            def body(r, c):
                getattr(copy_of(r), method)()
                return c
            return body
        for method in ("start", "wait"):
            for e in range(N_EXPERTS):
                lax.fori_loop(fill_lo_ref[e], fill_hi_ref[e], do(fill, method), 0)
            lax.fori_loop(n_used_ref[0], n_blocks, do(fill_blk, method), 0)


def _dispatch(fill_lo, fill_hi, n_used, dest_tiles, h2_tiles, n_rows):
    t = h2_tiles.shape[0] // TILE_ROWS
    tm = ROW_TILE
    grid_spec = pltpu.PrefetchScalarGridSpec(
        num_scalar_prefetch=3,
        grid=(t // tm,),
        in_specs=[
            pl.BlockSpec((1, 1, TOP_K * tm), lambda i, lo, hi, nu: (i, 0, 0), memory_space=pltpu.SMEM),
            pl.BlockSpec((tm * TILE_ROWS, LANES), lambda i, lo, hi, nu: (i, 0)),
        ],
        out_specs=pl.BlockSpec(memory_space=pl.ANY),
        scratch_shapes=[
            pltpu.VMEM((2, tm * TILE_ROWS, LANES), F32),
            pltpu.VMEM((MOE_ROWS * TILE_ROWS, LANES), F32),
            pltpu.SemaphoreType.DMA((2,)),
            pltpu.SemaphoreType.DMA,
            pltpu.SemaphoreType.DMA,
        ],
    )
    return pl.pallas_call(
        functools.partial(_dispatch_body, tm=tm, n_blocks=n_rows // MOE_ROWS),
        grid_spec=grid_spec,
        out_shape=jax.ShapeDtypeStruct((n_rows * TILE_ROWS, LANES), F32),
        compiler_params=pltpu.CompilerParams(dimension_semantics=("arbitrary",), vmem_limit_bytes=VMEM_LIMIT),
        name="dispatch",
    )(fill_lo, fill_hi, n_used, dest_tiles, h2_tiles)


def _moe_body(blk_exp_ref, x_ref, wg_ref, wu_ref, wd_ref, y_ref, wg_b, wu_b, wd_b):
    i = pl.program_id(0)

    @pl.when((i == 0) | (blk_exp_ref[i] != blk_exp_ref[jnp.maximum(i - 1, 0)]))
    def _():
        wg_b[...] = wg_ref[0].astype(BF16)
        wu_b[...] = wu_ref[0].astype(BF16)
        wd_b[...] = wd_ref[0].astype(BF16)

    xb = _from_row_tiles(x_ref, 0, MOE_ROWS).astype(BF16)
    g = jnp.dot(xb, wg_b[...], preferred_element_type=F32)
    u = jnp.dot(xb, wu_b[...], preferred_element_type=F32)
    act = (g * jax.nn.sigmoid(g) * u).astype(BF16)
    _to_row_tiles(y_ref, jnp.dot(act, wd_b[...], preferred_element_type=F32))


def _moe_ffn(blk_exp, buf, wg, wu, wd):
    n_rows = buf.shape[0] // TILE_ROWS
    d, f = wg.shape[1:]
    rows = MOE_ROWS
    row_blk = lambda i, be: (i, 0)
    w_blk = lambda i, be: (be[i], 0, 0)
    grid_spec = pltpu.PrefetchScalarGridSpec(
        num_scalar_prefetch=1,
        grid=(n_rows // rows,),
        in_specs=[
            pl.BlockSpec((rows * TILE_ROWS, LANES), row_blk),
            pl.BlockSpec((1, d, f), w_blk),
            pl.BlockSpec((1, d, f), w_blk),
            pl.BlockSpec((1, f, d), w_blk),
        ],
        out_specs=pl.BlockSpec((rows * TILE_ROWS, LANES), row_blk),
        scratch_shapes=[pltpu.VMEM((d, f), BF16), pltpu.VMEM((d, f), BF16), pltpu.VMEM((f, d), BF16)],
    )
    return pl.pallas_call(
        _moe_body,
        grid_spec=grid_spec,
        out_shape=jax.ShapeDtypeStruct(buf.shape, F32),
        compiler_params=pltpu.CompilerParams(dimension_semantics=("arbitrary",), vmem_limit_bytes=VMEM_LIMIT),
        name="moe_ffn",
    )(blk_exp, buf, wg, wu, wd)


def _finish_body(src_ref, src_next_ref, x1_ref, rt_ref, p_ref, pn_ref, wg_ref, bg_ref, wp_ref, fn_ref, y_hbm,
                 o_ref, ybuf, sem, *, last_layer):
    tm = x1_ref.shape[0]
    i = pl.program_id(0)
    n = pl.num_programs(0)
    slot = i % 2

    def row_copy(src, r, s):
        return pltpu.make_async_copy(_tile_of(y_hbm, src), _tile_of(ybuf.at[s], r), sem.at[s])

    def start_gather(idx_ref, s):
        def body(g, c):
            r0 = g * (TOP_K * DMA_GROUP)
            src = [idx_ref[0, 0, r0 + u] for u in range(TOP_K * DMA_GROUP)]
            for u in range(TOP_K * DMA_GROUP):
                row_copy(src[u], r0 + u, s).start(priority=u % 2)
            return c
        lax.fori_loop(0, tm // DMA_GROUP, body, 0)

    @pl.when(i == 0)
    def _():
        start_gather(src_ref, slot)

    @pl.when(i + 1 < n)
    def _():
        start_gather(src_next_ref, 1 - slot)

    _wait_copies(lambda r: row_copy(0, 0, slot), TOP_K * tm)

    rt = rt_ref[...]
    y0 = _from_row_tiles(ybuf.at[slot], 0, tm)
    y1 = _from_row_tiles(ybuf.at[slot], tm, tm)
    x2 = x1_ref[...] + rt[:, RT_GATE:RT_GATE + 1] * y0 + rt[:, RT_GATE + 1:RT_GATE + 2] * y1
    hg = _rms(x2, pn_ref[...]).astype(BF16)
    gate = jax.nn.sigmoid(jnp.dot(hg, wg_ref[...], preferred_element_type=F32) + bg_ref[...])
    x3 = x2 + gate * jnp.dot(p_ref[...].astype(BF16), wp_ref[...], preferred_element_type=F32)
    o_ref[...] = _rms(x3, fn_ref[...]) if last_layer else x3


def _finish(dest_tiles, x1, rt, p2d, ple_norm, w_gate_b, b_gate, w_proj_b, final_norm, y_tiles, last_layer):
    t, d = x1.shape
    tm = ROW_TILE
    n_tiles = t // tm
    row = lambda i: (i, 0)
    full2 = lambda i: (0, 0)
    return pl.pallas_call(
        functools.partial(_finish_body, last_layer=last_layer),
        grid=(n_tiles,),
        in_specs=[
            pl.BlockSpec((1, 1, TOP_K * tm), lambda i: (i, 0, 0), memory_space=pltpu.SMEM),
            pl.BlockSpec((1, 1, TOP_K * tm), lambda i: (jnp.minimum(i + 1, n_tiles - 1), 0, 0),
                         memory_space=pltpu.SMEM),
            pl.BlockSpec((tm, d), row),
            pl.BlockSpec((tm, LANES), row),
            pl.BlockSpec((tm, p2d.shape[1]), row),
            pl.BlockSpec((1, d), full2),
            pl.BlockSpec(w_gate_b.shape, full2),
            pl.BlockSpec((1, d), full2),
            pl.BlockSpec(w_proj_b.shape, full2),
            pl.BlockSpec((1, d), full2),
            pl.BlockSpec(memory_space=pl.ANY),
        ],
        out_specs=pl.BlockSpec((tm, d), row),
        out_shape=jax.ShapeDtypeStruct((t, d), F32),
        scratch_shapes=[
            pltpu.VMEM((2, TOP_K * tm * TILE_ROWS, LANES), F32),
            pltpu.SemaphoreType.DMA((2,)),
        ],
        compiler_params=pltpu.CompilerParams(dimension_semantics=("arbitrary",), vmem_limit_bytes=VMEM_LIMIT),
        name="finish",
    )(dest_tiles, dest_tiles, x1, rt, p2d, ple_norm, w_gate_b, b_gate, w_proj_b, final_norm, y_tiles)


def _qk_column_perm():
    perm = np.zeros((DA_WIDTH,), np.int32)
    for h in range(DA_HEADS):
        for half in range(2):
            for mp in range(2):
                for i in range(ROPE_HALF):
                    perm[h * LANES + half * 64 + mp * ROPE_HALF + i] = h * LANES + mp * DA_HEAD_DIM + half * ROPE_HALF + i
    return perm


def _rope_tables(seq):
    inv = 1.0 / (ROPE_THETA ** (jnp.arange(0, DA_HEAD_DIM, 2, dtype=F32) / DA_HEAD_DIM))
    ang = jnp.arange(seq, dtype=F32)[:, None] * inv[None, :]
    cos = jnp.tile(jnp.cos(ang), (1, LANES // ROPE_HALF))
    sin = jnp.tile(jnp.sin(ang), (1, LANES // ROPE_HALF))
    sign = jnp.where(jnp.arange(LANES) < LANES // 2, -1.0, 1.0).astype(F32)
    return cos, sin * sign[None, :]


def _dispatch_tables(rt, counts_f, n_tokens):
    rows = MOE_ROWS
    tm = ROW_TILE
    counts = counts_f[0, ROUTER_EXPERT_LANE0:ROUTER_EXPERT_LANE0 + N_EXPERTS].astype(jnp.int32)
    padded = ((counts + rows - 1) // rows) * rows
    pend = jnp.cumsum(padded)
    pstart = pend - padded
    n_blocks = (n_tokens * TOP_K + N_EXPERTS * rows) // rows
    n_used = (pend[-1:] // rows).astype(jnp.int32)
    blk_start = jnp.arange(n_blocks, dtype=jnp.int32) * rows
    blk_exp = jnp.minimum(jnp.sum(pend[None, :] <= blk_start[:, None], axis=1), N_EXPERTS - 1).astype(jnp.int32)
    expert = rt[:, RT_EXPERT:RT_EXPERT + TOP_K].astype(jnp.int32)
    rank = rt[:, RT_RANK:RT_RANK + TOP_K].astype(jnp.int32)
    first_row = jnp.sum(jnp.where(expert[..., None] == jnp.arange(N_EXPERTS), pstart, 0), axis=-1)
    dest = first_row + rank
    dest_tiles = dest.reshape(n_tokens // tm, tm, TOP_K).transpose(0, 2, 1).reshape(n_tokens // tm, 1, TOP_K * tm)
    return blk_exp, n_used, (pstart + counts).astype(jnp.int32), pend.astype(jnp.int32), dest_tiles, n_blocks * rows


def _layer(x, p_i, i, last_layer, final_norm, attn_norm, w_in, lambda_q1, lambda_k1, lambda_q2, lambda_k2,
           diff_subln, gm_ln_gain, gm_ln_bias, gm_spatial_w, gm_spatial_b, gm_out_norm, w_out, moe_norm,
           w_group_router, w_expert_router, w_expert_gate, w_expert_up, w_expert_down, ple_norm, w_ple_gate,
           b_ple_gate, w_ple_proj):
    batch, seq, d = x.shape
    t = batch * seq
    lambda_init = 0.8 - 0.6 * math.exp(-0.3 * i)
    x2d = x.reshape(t, d)

    perm = _qk_column_perm()
    w_in_b = jnp.concatenate(
        [w_in[:, perm], w_in[:, DA_WIDTH + perm], w_in[:, 2 * DA_WIDTH:]], axis=1).astype(BF16)
    cos_t, sin_t = _rope_tables(seq)
    qt5, k, vt5, o_gm = _in_proj(x2d, attn_norm[None, :], w_in_b, cos_t, sin_t, gm_ln_gain[None, :],
                                 gm_ln_bias[None, :], gm_spatial_w, gm_spatial_b.T, gm_out_norm[None, :], seq)
    lam_params = jnp.stack([lambda_q1, lambda_k1, lambda_q2, lambda_k2]).astype(F32)
    o_da = _attention(lam_params, diff_subln[:, None], qt5, k, vt5, batch, seq, lambda_init)

    w_router = jnp.concatenate([w_group_router, w_expert_router], axis=1)
    w_router_b = jnp.pad(w_router, ((0, 0), (0, LANES - w_router.shape[1]))).astype(BF16)
    x1, h2_tiles, rt, counts_f = _out_proj(x2d, o_da, o_gm, w_out.astype(BF16), moe_norm[None, :], w_router_b)

    blk_exp, n_used, fill_lo, fill_hi, dest_tiles, n_rows = _dispatch_tables(rt, counts_f, t)
    buf = _dispatch(fill_lo, fill_hi, n_used, dest_tiles, h2_tiles, n_rows)
    y_tiles = _moe_ffn(blk_exp, buf, w_expert_gate, w_expert_up, w_expert_down)

    out = _finish(dest_tiles, x1, rt, p_i.reshape(t, -1), ple_norm[None, :], w_ple_gate.astype(BF16),
                  b_ple_gate[None, :], w_ple_proj.astype(BF16), final_norm[None, :], y_tiles, last_layer)
    return out.reshape(batch, seq, d)


def kernel(x, p, attn_norm, w_in, lambda_q1, lambda_k1, lambda_q2, lambda_k2, diff_subln, gm_ln_gain, gm_ln_bias, gm_spatial_w, gm_spatial_b, gm_out_norm, w_out, moe_norm, w_group_router, w_expert_router, w_expert_gate, w_expert_up, w_expert_down, ple_norm, w_ple_gate, b_ple_gate, w_ple_proj, final_norm):
    per_layer = (attn_norm, w_in, lambda_q1, lambda_k1, lambda_q2, lambda_k2, diff_subln, gm_ln_gain, gm_ln_bias,
                 gm_spatial_w, gm_spatial_b, gm_out_norm, w_out, moe_norm, w_group_router, w_expert_router,
                 w_expert_gate, w_expert_up, w_expert_down, ple_norm, w_ple_gate, b_ple_gate, w_ple_proj)
    depth = attn_norm.shape[0]
    for i in range(depth):
        x = _layer(x, p[i], i, i == depth - 1, final_norm, *(w[i] for w in per_layer))
    return x
```

```python
import functools
import math

import numpy as np
import jax
import jax.numpy as jnp
from jax import lax
from jax.experimental import pallas as pl
from jax.experimental.pallas import tpu as pltpu

F32 = jnp.float32
BF16 = jnp.bfloat16

DA_HEADS = 4
DA_HEAD_DIM = 64
DA_VDIM = 2 * DA_HEAD_DIM
DA_WIDTH = DA_HEADS * DA_VDIM
ROPE_THETA = 10000.0
GM_HEADS = 4
GM_HEAD_DIM = 128
GM_WIDTH = GM_HEADS * GM_HEAD_DIM
GM_CHUNK = 128
N_GROUPS = 4
EXPERTS_PER_GROUP = 8
N_EXPERTS = N_GROUPS * EXPERTS_PER_GROUP
TOP_K = 2
EPS = 1e-6

LANES = 128
ROPE_HALF = DA_HEAD_DIM // 2
NEG_BIG = -1e30

ROW_TILE = 512
ATT_TQ = 256
ATT_TK = 512
V_ROWS = DA_VDIM + 16
MOE_ROWS = 512
VMEM_LIMIT = 56 * 1024 * 1024


def _rms(x, g):
    return x * lax.rsqrt(jnp.mean(x * x, axis=-1, keepdims=True) + EPS) * g


def _gelu_tanh(x):
    c = math.sqrt(2.0 / math.pi)
    return x * (0.5 * (1.0 + jnp.tanh(c * (x + 0.044715 * (x * x * x)))))


def _in_proj_body(x_ref, g_ref, w_ref, cos_ref, sin_ref, lng_ref, lnb_ref, ws_ref, bst_ref, og_ref,
                  qt_ref, k_ref, vt_ref, ogm_ref, mix_ref):
    tm = x_ref.shape[0]
    h = _rms(x_ref[...], g_ref[...]).astype(BF16)
    cos = cos_ref[...]
    sin = sin_ref[...]

    zq = jnp.dot(h, w_ref[:, 0:DA_WIDTH], preferred_element_type=F32)
    zk = jnp.dot(h, w_ref[:, DA_WIDTH:2 * DA_WIDTH], preferred_element_type=F32)
    scale = math.log2(math.e) / math.sqrt(DA_HEAD_DIM)
    for hd in range(DA_HEADS):
        sl = slice(hd * LANES, (hd + 1) * LANES)
        qh = zq[:, sl]
        kh = zk[:, sl]
        q_rot = (qh * cos + pltpu.roll(qh, LANES // 2, 1) * sin) * scale
        for c in range(qt_ref.shape[1]):
            qt_ref[hd, c] = q_rot[c * ATT_TQ:(c + 1) * ATT_TQ, :].T.astype(BF16)
        k_ref[:, sl] = (kh * cos + pltpu.roll(kh, LANES // 2, 1) * sin).astype(BF16)

    zv = jnp.dot(h, w_ref[:, 2 * DA_WIDTH:3 * DA_WIDTH], preferred_element_type=F32)
    for hd in range(DA_HEADS):
        vt_ref[hd, 0:DA_VDIM, :] = zv[:, hd * DA_VDIM:(hd + 1) * DA_VDIM].T.astype(BF16)
        vt_ref[hd, DA_VDIM:V_ROWS, :] = jnp.ones((V_ROWS - DA_VDIM, tm), BF16)

    a = _gelu_tanh(jnp.dot(h, w_ref[:, 3 * DA_WIDTH:3 * DA_WIDTH + 2 * GM_WIDTH], preferred_element_type=F32))
    u = a[:, :GM_WIDTH]
    vv = a[:, GM_WIDTH:]
    mu = jnp.mean(vv, axis=-1, keepdims=True)
    vc = vv - mu
    vln = vc * lax.rsqrt(jnp.mean(vc * vc, axis=-1, keepdims=True) + EPS) * lng_ref[...] + lnb_ref[...]
    vb = vln.astype(BF16)
    row = lax.broadcasted_iota(jnp.int32, (GM_CHUNK, GM_CHUNK), 0)
    col = lax.broadcasted_iota(jnp.int32, (GM_CHUNK, GM_CHUNK), 1)
    causal = row >= col
    for hd in range(GM_HEADS):
        wm = jnp.where(causal, ws_ref[hd], 0.0).astype(BF16)
        bias = bst_ref[:, hd:hd + 1]
        cs = slice(hd * GM_HEAD_DIM, (hd + 1) * GM_HEAD_DIM)
        for c in range(tm // GM_CHUNK):
            rs = slice(c * GM_CHUNK, (c + 1) * GM_CHUNK)
            mixed = jnp.dot(wm, vb[rs, cs], preferred_element_type=F32) + bias
            mix_ref[rs, cs] = u[rs, cs] * mixed
    ogm_ref[...] = _rms(mix_ref[...], og_ref[...]).astype(BF16)


def _in_proj(x2d, attn_norm, w_in_b, cos_t, sin_t, ln_g, ln_b, ws, bs_t, out_g, seq):
    t, d = x2d.shape
    tm = ROW_TILE
    assert tm % ATT_TQ == 0 and ATT_TK % tm == 0
    batch = t // seq
    n_pos = seq // tm
    tiles_per_kblk = ATT_TK // tm
    row = lambda i: (i, 0)
    full2 = lambda i: (0, 0)
    return pl.pallas_call(
        _in_proj_body,
        grid=(t // tm,),
        in_specs=[
            pl.BlockSpec((tm, d), row),
            pl.BlockSpec((1, d), full2),
            pl.BlockSpec(w_in_b.shape, full2),
            pl.BlockSpec((tm, LANES), lambda i: (i % n_pos, 0)),
            pl.BlockSpec((tm, LANES), lambda i: (i % n_pos, 0)),
            pl.BlockSpec((1, GM_WIDTH), full2),
            pl.BlockSpec((1, GM_WIDTH), full2),
            pl.BlockSpec(ws.shape, lambda i: (0, 0, 0)),
            pl.BlockSpec(bs_t.shape, full2),
            pl.BlockSpec((1, GM_WIDTH), full2),
        ],
        out_specs=[
            pl.BlockSpec((None, DA_HEADS, tm // ATT_TQ, DA_VDIM, ATT_TQ),
                         lambda i: (i // n_pos, 0, i % n_pos, 0, 0)),
            pl.BlockSpec((tm, DA_WIDTH), row),
            pl.BlockSpec((None, DA_HEADS, None, V_ROWS, tm),
                         lambda i: (i // n_pos, 0, (i % n_pos) // tiles_per_kblk, 0, (i % n_pos) % tiles_per_kblk)),
            pl.BlockSpec((tm, GM_WIDTH), row),
        ],
        out_shape=[
            jax.ShapeDtypeStruct((batch, DA_HEADS, seq // ATT_TQ, DA_VDIM, ATT_TQ), BF16),
            jax.ShapeDtypeStruct((t, DA_WIDTH), BF16),
            jax.ShapeDtypeStruct((batch, DA_HEADS, seq // ATT_TK, V_ROWS, ATT_TK), BF16),
            jax.ShapeDtypeStruct((t, GM_WIDTH), BF16),
        ],
        scratch_shapes=[pltpu.VMEM((tm, GM_WIDTH), F32)],
        compiler_params=pltpu.CompilerParams(dimension_semantics=("parallel",), vmem_limit_bytes=VMEM_LIMIT),
        name="in_proj",
    )(x2d, attn_norm, w_in_b, cos_t, sin_t, ln_g, ln_b, ws, bs_t, out_g)


def _attn_body(lam_ref, g_ref, qt_ref, qtn_ref, k_ref, vt_ref, o_ref, *scratch, tq, tk, lambda_init):
    i = pl.program_id(1)
    nh = qt_ref.shape[0]
    qst_refs, qsn_refs, m_refs, acc_refs, s_refs, bmax_refs = (scratch[n * nh:(n + 1) * nh] for n in range(6))
    cap_ref = scratch[6 * nh]
    caps = cap_ref.shape[0]
    lp = lam_ref[...]
    lam = (jnp.exp(jnp.sum(lp[0:1] * lp[1:2], axis=-1, keepdims=True))
           - jnp.exp(jnp.sum(lp[2:3] * lp[3:4], axis=-1, keepdims=True)) + lambda_init)

    feat = lax.broadcasted_iota(jnp.int32, (DA_VDIM, tq), 0)
    is_map1 = (feat % DA_HEAD_DIM) < ROPE_HALF
    for h in range(nh):
        for src, dst in ((qt_ref, qst_refs), (qtn_ref, qsn_refs)):
            qt = src[h]
            zero = jnp.zeros_like(qt)
            dst[h][:, 0:tq] = jnp.where(is_map1, qt, zero)
            dst[h][:, tq:2 * tq] = jnp.where(is_map1, zero, qt)
        m_refs[h][...] = jnp.full(m_refs[h].shape, NEG_BIG, F32)
        acc_refs[h][...] = jnp.zeros(acc_refs[h].shape, F32)

    def produce_scores(h, q_refs, j):
        row0 = pl.multiple_of(j * tk, tk)
        kj = k_ref[pl.ds(row0, tk), h * LANES:(h + 1) * LANES]
        sc = jnp.dot(kj, q_refs[h][...], preferred_element_type=F32)
        s_refs[h][...] = sc
        bmax_refs[h][...] = jnp.max(sc, axis=0, keepdims=True)

    def step(j, masked):
        for h in range(nh):
            s = s_refs[h][...]
            if masked:
                s = jnp.minimum(s, cap_ref[i % caps])
                bmax = jnp.max(s, axis=0, keepdims=True)
            else:
                bmax = bmax_refs[h][...]
            m_old = m_refs[h][...]
            m_new = jnp.maximum(m_old, bmax)
            alpha = jnp.exp2(m_old - m_new)
            p = jnp.exp2(s - m_new)
            m_refs[h][...] = m_new
            if masked:
                produce_scores(h, qsn_refs, 0)
            else:
                produce_scores(h, qst_refs, j + 1)
            acc_refs[h][...] = (acc_refs[h][...] * alpha
                                + jnp.dot(vt_ref[h, j], p.astype(BF16), preferred_element_type=F32))

    n_full = (i * tq) // tk

    @pl.when(i == 0)
    def _():
        for h in range(nh):
            produce_scores(h, qst_refs, 0)
        r_i = lax.broadcasted_iota(jnp.int32, (tk, 2 * tq), 0)
        c_i = lax.broadcasted_iota(jnp.int32, (tk, 2 * tq), 1) % tq
        for par in range(caps):
            cap_ref[par] = jnp.where(r_i - c_i <= par * tq, -NEG_BIG, NEG_BIG)

    def full_step(j, c):
        step(j, False)
        return c
    lax.fori_loop(0, n_full, full_step, 0)
    step(n_full, True)

    for h in range(nh):
        o = acc_refs[h][0:DA_VDIM, :] * (1.0 / acc_refs[h][DA_VDIM:DA_VDIM + 1, :])
        od = o[:, :tq] - lam * o[:, tq:]
        y = od * lax.rsqrt(jnp.mean(od * od, axis=0, keepdims=True) + EPS) * g_ref[...] * (1.0 - lambda_init)
        o_ref[:, h * LANES:(h + 1) * LANES] = y.T.astype(BF16)


def _attention(lam_params, subln_col, qt5, k2d, vt5, batch, seq, lambda_init):
    tq, tk = ATT_TQ, ATT_TK
    nq, nk = seq // tq, seq // tk
    nh = DA_HEADS
    body = functools.partial(_attn_body, tq=tq, tk=tk, lambda_init=lambda_init)

    def next_q(b, i):
        f = jnp.minimum(b * nq + i + 1, batch * nq - 1)
        return (f // nq, 0, f % nq, 0, 0)
    return pl.pallas_call(
        body,
        grid=(batch, nq),
        in_specs=[
            pl.BlockSpec(lam_params.shape, lambda b, i: (0, 0)),
            pl.BlockSpec(subln_col.shape, lambda b, i: (0, 0)),
            pl.BlockSpec((None, nh, None, DA_VDIM, tq), lambda b, i: (b, 0, i, 0, 0)),
            pl.BlockSpec((None, nh, None, DA_VDIM, tq), next_q),
            pl.BlockSpec((seq, DA_WIDTH), lambda b, i: (b, 0)),
            pl.BlockSpec((None, nh, nk, V_ROWS, tk), lambda b, i: (b, 0, 0, 0, 0)),
        ],
        out_specs=pl.BlockSpec((tq, DA_WIDTH), lambda b, i: (b * nq + i, 0)),
        out_shape=jax.ShapeDtypeStruct((batch * seq, DA_WIDTH), BF16),
        scratch_shapes=(
            [pltpu.VMEM((DA_VDIM, 2 * tq), BF16)] * (2 * nh)
            + [pltpu.VMEM((1, 2 * tq), F32)] * nh
            + [pltpu.VMEM((V_ROWS, 2 * tq), F32)] * nh
            + [pltpu.VMEM((tk, 2 * tq), F32)] * nh
            + [pltpu.VMEM((1, 2 * tq), F32)] * nh
            + [pltpu.VMEM((tk // tq, tk, 2 * tq), F32)]),
        compiler_params=pltpu.CompilerParams(
            dimension_semantics=("arbitrary", "arbitrary"), vmem_limit_bytes=VMEM_LIMIT),
        name="diff_attention",
    )(lam_params, subln_col, qt5, qt5, k2d, vt5)


RT_GATE = 0
RT_EXPERT = 2
RT_RANK = 4
ROUTER_EXPERT_LANE0 = N_GROUPS


TILE_ROWS = 8


def _to_row_tiles(ref, value):
    rows = value.shape[0]
    for c in range(TILE_ROWS):
        ref[pl.ds(c, rows, stride=TILE_ROWS), :] = value[:, c * LANES:(c + 1) * LANES]


def _from_row_tiles(ref, row0, rows):
    return jnp.concatenate(
        [ref[pl.ds(row0 * TILE_ROWS + c, rows, stride=TILE_ROWS), :] for c in range(TILE_ROWS)], axis=1)


def _tile_of(ref, row):
    return ref.at[pl.ds(pl.multiple_of(row * TILE_ROWS, TILE_ROWS), TILE_ROWS)]


def _out_proj_body(x_ref, oda_ref, ogm_ref, w_ref, g_ref, wr_ref, x1_ref, h2_ref, rt_ref, cnt_ref, carry_ref):
    tm = x_ref.shape[0]

    @pl.when(pl.program_id(0) == 0)
    def _():
        carry_ref[...] = jnp.zeros(carry_ref.shape, F32)

    x1 = (x_ref[...]
          + jnp.dot(oda_ref[...], w_ref[0:DA_WIDTH, :], preferred_element_type=F32)
          + jnp.dot(ogm_ref[...], w_ref[DA_WIDTH:DA_WIDTH + GM_WIDTH, :], preferred_element_type=F32))
    x1_ref[...] = x1
    h2 = _rms(x1, g_ref[...])
    _to_row_tiles(h2_ref, h2)
    lg = jnp.dot(h2.astype(BF16), wr_ref[...], preferred_element_type=F32)

    lane = lax.broadcasted_iota(jnp.int32, lg.shape, 1).astype(F32)
    ninf = -jnp.inf
    first_lane = lambda hit: jnp.min(jnp.where(hit, lane, float(LANES)), axis=-1, keepdims=True)
    is_grp = lane < N_GROUPS
    gl = jnp.where(is_grp, lg, ninf)
    gmax = jnp.max(gl, axis=-1, keepdims=True)
    g_idx = first_lane(gl == gmax)
    g_gate = 1.0 / jnp.sum(jnp.where(is_grp, jnp.exp(lg - gmax), 0.0), axis=-1, keepdims=True)
    e_of_lane = lane - ROUTER_EXPERT_LANE0
    e_lo = g_idx * EXPERTS_PER_GROUP
    in_grp = (e_of_lane >= e_lo) & (e_of_lane < e_lo + EXPERTS_PER_GROUP)
    el = jnp.where(in_grp, lg, ninf)
    v1 = jnp.max(el, axis=-1, keepdims=True)
    i1 = first_lane(el == v1)
    el2 = jnp.where(lane == i1, ninf, el)
    v2 = jnp.max(el2, axis=-1, keepdims=True)
    i2 = first_lane(el2 == v2)
    t21 = jnp.exp(v2 - v1)
    w1 = g_gate / (1.0 + t21)
    w2 = g_gate * t21 / (1.0 + t21)

    hit1 = lane == i1
    hit2 = lane == i2
    chosen = (hit1 | hit2).astype(BF16)
    r_i = lax.broadcasted_iota(jnp.int32, (tm, tm), 0)
    c_i = lax.broadcasted_iota(jnp.int32, (tm, tm), 1)
    before = jnp.where(r_i > c_i, 1.0, 0.0).astype(BF16)
    base = carry_ref[...] + jnp.dot(before, chosen, preferred_element_type=F32)
    rank1 = jnp.sum(jnp.where(hit1, base, 0.0), axis=-1, keepdims=True)
    rank2 = jnp.sum(jnp.where(hit2, base, 0.0), axis=-1, keepdims=True)
    carry_ref[...] = carry_ref[...] + jnp.sum(chosen.astype(F32), axis=0, keepdims=True)
    cnt_ref[...] = carry_ref[...]

    rec = jnp.zeros(lg.shape, F32)
    for k, val in ((RT_GATE, w1), (RT_GATE + 1, w2),
                   (RT_EXPERT, (i1 - ROUTER_EXPERT_LANE0).astype(F32)),
                   (RT_EXPERT + 1, (i2 - ROUTER_EXPERT_LANE0).astype(F32)),
                   (RT_RANK, rank1), (RT_RANK + 1, rank2)):
        rec = jnp.where(lane == k, val, rec)
    rt_ref[...] = rec


def _out_proj(x2d, o_da, o_gm, w_out_b, moe_norm, w_router_b):
    t, d = x2d.shape
    tm = ROW_TILE
    row = lambda i: (i, 0)
    full2 = lambda i: (0, 0)
    return pl.pallas_call(
        _out_proj_body,
        grid=(t // tm,),
        in_specs=[
            pl.BlockSpec((tm, d), row),
            pl.BlockSpec((tm, DA_WIDTH), row),
            pl.BlockSpec((tm, GM_WIDTH), row),
            pl.BlockSpec(w_out_b.shape, full2),
            pl.BlockSpec((1, d), full2),
            pl.BlockSpec(w_router_b.shape, full2),
        ],
        out_specs=[
            pl.BlockSpec((tm, d), row),
            pl.BlockSpec((tm * TILE_ROWS, LANES), row),
            pl.BlockSpec((tm, LANES), row),
            pl.BlockSpec((1, LANES), full2),
        ],
        out_shape=[
            jax.ShapeDtypeStruct((t, d), F32),
            jax.ShapeDtypeStruct((t * TILE_ROWS, LANES), F32),
            jax.ShapeDtypeStruct((t, LANES), F32),
            jax.ShapeDtypeStruct((1, LANES), F32),
        ],
        scratch_shapes=[pltpu.VMEM((1, LANES), F32)],
        compiler_params=pltpu.CompilerParams(dimension_semantics=("arbitrary",), vmem_limit_bytes=VMEM_LIMIT),
        name="out_proj",
    )(x2d, o_da, o_gm, w_out_b, moe_norm, w_router_b)


DMA_GROUP = 8


def _wait_copies(make_copy, count):
    def body(r, c):
        make_copy(r).wait()
        return c
    lax.fori_loop(0, count, body, 0, unroll=8)


def _dispatch_body(fill_lo_ref, fill_hi_ref, n_used_ref, dst_ref, h2_ref, buf_hbm, stage, zblk, sem, zsem, bsem,
                   *, tm, n_blocks):
    i = pl.program_id(0)
    n = pl.num_programs(0)
    slot = i % 2
    stage[slot] = h2_ref[...]

    def row_copy(r, dst, s):
        return pltpu.make_async_copy(_tile_of(stage.at[s], r), _tile_of(buf_hbm, dst), sem.at[s])

    def issue(g, c):
        r0 = g * DMA_GROUP
        dst = [[dst_ref[0, 0, k * tm + r0 + u] for k in range(TOP_K)] for u in range(DMA_GROUP)]
        for u in range(DMA_GROUP):
            for k in range(TOP_K):
                row_copy(r0 + u, dst[u][k], slot).start(priority=k % 2)
        return c
    lax.fori_loop(0, tm // DMA_GROUP, issue, 0)

    def wait_tile(s):
        _wait_copies(lambda r: row_copy(0, 0, s), TOP_K * tm)

    @pl.when(i > 0)
    def _():
        wait_tile(1 - slot)

    @pl.when(i == n - 1)
    def _():
        wait_tile(slot)
        zblk[...] = jnp.zeros(zblk.shape, F32)
        fill = lambda r: pltpu.make_async_copy(zblk.at[pl.ds(0, TILE_ROWS)], _tile_of(buf_hbm, r), zsem)
        blk_tiles = zblk.shape[0]
        fill_blk = lambda b: pltpu.make_async_copy(
            zblk, buf_hbm.at[pl.ds(pl.multiple_of(b * blk_tiles, blk_tiles), blk_tiles)], bsem)

        def do(copy_of, method):
            def body(r, c):
                getattr(copy_of(r), method)()
                return c
            return body
        for method in ("start", "wait"):
            for e in range(N_EXPERTS):
                lax.fori_loop(fill_lo_ref[e], fill_hi_ref[e], do(fill, method), 0)
            lax.fori_loop(n_used_ref[0], n_blocks, do(fill_blk, method), 0)


def _dispatch(fill_lo, fill_hi, n_used, dest_tiles, h2_tiles, n_rows):
    t = h2_tiles.shape[0] // TILE_ROWS
    tm = ROW_TILE
    grid_spec = pltpu.PrefetchScalarGridSpec(
        num_scalar_prefetch=3,
        grid=(t // tm,),
        in_specs=[
            pl.BlockSpec((1, 1, TOP_K * tm), lambda i, lo, hi, nu: (i, 0, 0), memory_space=pltpu.SMEM),
            pl.BlockSpec((tm * TILE_ROWS, LANES), lambda i, lo, hi, nu: (i, 0)),
        ],
        out_specs=pl.BlockSpec(memory_space=pl.ANY),
        scratch_shapes=[
            pltpu.VMEM((2, tm * TILE_ROWS, LANES), F32),
            pltpu.VMEM((MOE_ROWS * TILE_ROWS, LANES), F32),
            pltpu.SemaphoreType.DMA((2,)),
            pltpu.SemaphoreType.DMA,
            pltpu.SemaphoreType.DMA,
        ],
    )
    return pl.pallas_call(
        functools.partial(_dispatch_body, tm=tm, n_blocks=n_rows // MOE_ROWS),
        grid_spec=grid_spec,
        out_shape=jax.ShapeDtypeStruct((n_rows * TILE_ROWS, LANES), F32),
        compiler_params=pltpu.CompilerParams(dimension_semantics=("arbitrary",), vmem_limit_bytes=VMEM_LIMIT),
        name="dispatch",
    )(fill_lo, fill_hi, n_used, dest_tiles, h2_tiles)


def _moe_body(blk_exp_ref, x_ref, wg_ref, wu_ref, wd_ref, y_ref, wg_b, wu_b, wd_b):
    i = pl.program_id(0)

    @pl.when((i == 0) | (blk_exp_ref[i] != blk_exp_ref[jnp.maximum(i - 1, 0)]))
    def _():
        wg_b[...] = wg_ref[0].astype(BF16)
        wu_b[...] = wu_ref[0].astype(BF16)
        wd_b[...] = wd_ref[0].astype(BF16)

    xb = _from_row_tiles(x_ref, 0, MOE_ROWS).astype(BF16)
    g = jnp.dot(xb, wg_b[...], preferred_element_type=F32)
    u = jnp.dot(xb, wu_b[...], preferred_element_type=F32)
    act = (g * jax.nn.sigmoid(g) * u).astype(BF16)
    _to_row_tiles(y_ref, jnp.dot(act, wd_b[...], preferred_element_type=F32))


def _moe_ffn(blk_exp, buf, wg, wu, wd):
    n_rows = buf.shape[0] // TILE_ROWS
    d, f = wg.shape[1:]
    rows = MOE_ROWS
    row_blk = lambda i, be: (i, 0)
    w_blk = lambda i, be: (be[i], 0, 0)
    grid_spec = pltpu.PrefetchScalarGridSpec(
        num_scalar_prefetch=1,
        grid=(n_rows // rows,),
        in_specs=[
            pl.BlockSpec((rows * TILE_ROWS, LANES), row_blk),
            pl.BlockSpec((1, d, f), w_blk),
            pl.BlockSpec((1, d, f), w_blk),
            pl.BlockSpec((1, f, d), w_blk),
        ],
        out_specs=pl.BlockSpec((rows * TILE_ROWS, LANES), row_blk),
        scratch_shapes=[pltpu.VMEM((d, f), BF16), pltpu.VMEM((d, f), BF16), pltpu.VMEM((f, d), BF16)],
    )
    return pl.pallas_call(
        _moe_body,
        grid_spec=grid_spec,
        out_shape=jax.ShapeDtypeStruct(buf.shape, F32),
        compiler_params=pltpu.CompilerParams(dimension_semantics=("arbitrary",), vmem_limit_bytes=VMEM_LIMIT),
        name="moe_ffn",
    )(blk_exp, buf, wg, wu, wd)


def _finish_body(src_ref, src_next_ref, x1_ref, rt_ref, p_ref, pn_ref, wg_ref, bg_ref, wp_ref, fn_ref, y_hbm,
                 o_ref, ybuf, sem, *, last_layer):
    tm = x1_ref.shape[0]
    i = pl.program_id(0)
    n = pl.num_programs(0)
    slot = i % 2

    def row_copy(src, r, s):
        return pltpu.make_async_copy(_tile_of(y_hbm, src), _tile_of(ybuf.at[s], r), sem.at[s])

    def gather_group(idx_ref, s, r0):
        src = [idx_ref[0, 0, r0 + u] for u in range(TOP_K * DMA_GROUP)]
        for u in range(TOP_K * DMA_GROUP):
            row_copy(src[u], r0 + u, s).start(priority=u % 2)

    @pl.when(i == 0)
    def _():
        def body(g, c):
            gather_group(src_ref, slot, g * (TOP_K * DMA_GROUP))
            return c
        lax.fori_loop(0, tm // DMA_GROUP, body, 0)

    _wait_copies(lambda r: row_copy(0, 0, slot), TOP_K * tm)

    for g in range(tm // DMA_GROUP):
        gather_group(src_next_ref, 1 - slot, g * (TOP_K * DMA_GROUP))

    rt = rt_ref[...]
    y0 = _from_row_tiles(ybuf.at[slot], 0, tm)
    y1 = _from_row_tiles(ybuf.at[slot], tm, tm)
    x2 = x1_ref[...] + rt[:, RT_GATE:RT_GATE + 1] * y0 + rt[:, RT_GATE + 1:RT_GATE + 2] * y1
    hg = _rms(x2, pn_ref[...]).astype(BF16)
    gate = jax.nn.sigmoid(jnp.dot(hg, wg_ref[...], preferred_element_type=F32) + bg_ref[...])
    x3 = x2 + gate * jnp.dot(p_ref[...].astype(BF16), wp_ref[...], preferred_element_type=F32)
    o_ref[...] = _rms(x3, fn_ref[...]) if last_layer else x3

    @pl.when(i == n - 1)
    def _():
        _wait_copies(lambda r: row_copy(0, 0, 1 - slot), TOP_K * tm)


def _finish(dest_tiles, x1, rt, p2d, ple_norm, w_gate_b, b_gate, w_proj_b, final_norm, y_tiles, last_layer):
    t, d = x1.shape
    tm = ROW_TILE
    n_tiles = t // tm
    row = lambda i: (i, 0)
    full2 = lambda i: (0, 0)
    return pl.pallas_call(
        functools.partial(_finish_body, last_layer=last_layer),
        grid=(n_tiles,),
        in_specs=[
            pl.BlockSpec((1, 1, TOP_K * tm), lambda i: (i, 0, 0), memory_space=pltpu.SMEM),
            pl.BlockSpec((1, 1, TOP_K * tm), lambda i: (jnp.minimum(i + 1, n_tiles - 1), 0, 0),
                         memory_space=pltpu.SMEM),
            pl.BlockSpec((tm, d), row),
            pl.BlockSpec((tm, LANES), row),
            pl.BlockSpec((tm, p2d.shape[1]), row),
            pl.BlockSpec((1, d), full2),
            pl.BlockSpec(w_gate_b.shape, full2),
            pl.BlockSpec((1, d), full2),
            pl.BlockSpec(w_proj_b.shape, full2),
            pl.BlockSpec((1, d), full2),
            pl.BlockSpec(memory_space=pl.ANY),
        ],
        out_specs=pl.BlockSpec((tm, d), row),
        out_shape=jax.ShapeDtypeStruct((t, d), F32),
        scratch_shapes=[
            pltpu.VMEM((2, TOP_K * tm * TILE_ROWS, LANES), F32),
            pltpu.SemaphoreType.DMA((2,)),
        ],
        compiler_params=pltpu.CompilerParams(dimension_semantics=("arbitrary",), vmem_limit_bytes=VMEM_LIMIT),
        name="finish",
    )(dest_tiles, dest_tiles, x1, rt, p2d, ple_norm, w_gate_b, b_gate, w_proj_b, final_norm, y_tiles)


def _qk_column_perm():
    perm = np.zeros((DA_WIDTH,), np.int32)
    for h in range(DA_HEADS):
        for half in range(2):
            for mp in range(2):
                for i in range(ROPE_HALF):
                    perm[h * LANES + half * 64 + mp * ROPE_HALF + i] = h * LANES + mp * DA_HEAD_DIM + half * ROPE_HALF + i
    return perm


def _rope_tables(seq):
    inv = 1.0 / (ROPE_THETA ** (jnp.arange(0, DA_HEAD_DIM, 2, dtype=F32) / DA_HEAD_DIM))
    ang = jnp.arange(seq, dtype=F32)[:, None] * inv[None, :]
    cos = jnp.tile(jnp.cos(ang), (1, LANES // ROPE_HALF))
    sin = jnp.tile(jnp.sin(ang), (1, LANES // ROPE_HALF))
    sign = jnp.where(jnp.arange(LANES) < LANES // 2, -1.0, 1.0).astype(F32)
    return cos, sin * sign[None, :]


def _dispatch_tables(rt, counts_f, n_tokens):
    rows = MOE_ROWS
    tm = ROW_TILE
    counts = counts_f[0, ROUTER_EXPERT_LANE0:ROUTER_EXPERT_LANE0 + N_EXPERTS].astype(jnp.int32)
    padded = ((counts + rows - 1) // rows) * rows
    pend = jnp.cumsum(padded)
    pstart = pend - padded
    n_blocks = (n_tokens * TOP_K + N_EXPERTS * rows) // rows
    n_used = (pend[-1:] // rows).astype(jnp.int32)
    blk_start = jnp.arange(n_blocks, dtype=jnp.int32) * rows
    blk_exp = jnp.minimum(jnp.sum(pend[None, :] <= blk_start[:, None], axis=1), N_EXPERTS - 1).astype(jnp.int32)
    expert = rt[:, RT_EXPERT:RT_EXPERT + TOP_K].astype(jnp.int32)
    rank = rt[:, RT_RANK:RT_RANK + TOP_K].astype(jnp.int32)
    first_row = jnp.sum(jnp.where(expert[..., None] == jnp.arange(N_EXPERTS), pstart, 0), axis=-1)
    dest = first_row + rank
    dest_tiles = dest.reshape(n_tokens // tm, tm, TOP_K).transpose(0, 2, 1).reshape(n_tokens // tm, 1, TOP_K * tm)
    return blk_exp, n_used, (pstart + counts).astype(jnp.int32), pend.astype(jnp.int32), dest_tiles, n_blocks * rows


def _layer(x, p_i, i, last_layer, final_norm, attn_norm, w_in, lambda_q1, lambda_k1, lambda_q2, lambda_k2,
           diff_subln, gm_ln_gain, gm_ln_bias, gm_spatial_w, gm_spatial_b, gm_out_norm, w_out, moe_norm,
           w_group_router, w_expert_router, w_expert_gate, w_expert_up, w_expert_down, ple_norm, w_ple_gate,
           b_ple_gate, w_ple_proj):
    batch, seq, d = x.shape
    t = batch * seq
    lambda_init = 0.8 - 0.6 * math.exp(-0.3 * i)
    x2d = x.reshape(t, d)

    perm = _qk_column_perm()
    w_in_b = jnp.concatenate(
        [w_in[:, perm], w_in[:, DA_WIDTH + perm], w_in[:, 2 * DA_WIDTH:]], axis=1).astype(BF16)
    cos_t, sin_t = _rope_tables(seq)
    qt5, k, vt5, o_gm = _in_proj(x2d, attn_norm[None, :], w_in_b, cos_t, sin_t, gm_ln_gain[None, :],
                                 gm_ln_bias[None, :], gm_spatial_w, gm_spatial_b.T, gm_out_norm[None, :], seq)
    lam_params = jnp.stack([lambda_q1, lambda_k1, lambda_q2, lambda_k2]).astype(F32)
    o_da = _attention(lam_params, diff_subln[:, None], qt5, k, vt5, batch, seq, lambda_init)

    w_router = jnp.concatenate([w_group_router, w_expert_router], axis=1)
    w_router_b = jnp.pad(w_router, ((0, 0), (0, LANES - w_router.shape[1]))).astype(BF16)
    x1, h2_tiles, rt, counts_f = _out_proj(x2d, o_da, o_gm, w_out.astype(BF16), moe_norm[None, :], w_router_b)

    blk_exp, n_used, fill_lo, fill_hi, dest_tiles, n_rows = _dispatch_tables(rt, counts_f, t)
    buf = _dispatch(fill_lo, fill_hi, n_used, dest_tiles, h2_tiles, n_rows)
    y_tiles = _moe_ffn(blk_exp, buf, w_expert_gate, w_expert_up, w_expert_down)

    out = _finish(dest_tiles, x1, rt, p_i.reshape(t, -1), ple_norm[None, :], w_ple_gate.astype(BF16),
                  b_ple_gate[None, :], w_ple_proj.astype(BF16), final_norm[None, :], y_tiles, last_layer)
    return out.reshape(batch, seq, d)


def kernel(x, p, attn_norm, w_in, lambda_q1, lambda_k1, lambda_q2, lambda_k2, diff_subln, gm_ln_gain, gm_ln_bias, gm_spatial_w, gm_spatial_b, gm_out_norm, w_out, moe_norm, w_group_router, w_expert_router, w_expert_gate, w_expert_up, w_expert_down, ple_norm, w_ple_gate, b_ple_gate, w_ple_proj, final_norm):
    per_layer = (attn_norm, w_in, lambda_q1, lambda_k1, lambda_q2, lambda_k2, diff_subln, gm_ln_gain, gm_ln_bias,
                 gm_spatial_w, gm_spatial_b, gm_out_norm, w_out, moe_norm, w_group_router, w_expert_router,
                 w_expert_gate, w_expert_up, w_expert_down, ple_norm, w_ple_gate, b_ple_gate, w_ple_proj)
    depth = attn_norm.shape[0]
    for i in range(depth):
        x = _layer(x, p[i], i, i == depth - 1, final_norm, *(w[i] for w in per_layer))
    return x
```

```python
import functools
import math

import numpy as np
import jax
import jax.numpy as jnp
from jax import lax
from jax.experimental import pallas as pl
from jax.experimental.pallas import tpu as pltpu

F32 = jnp.float32
BF16 = jnp.bfloat16

DA_HEADS = 4
DA_HEAD_DIM = 64
DA_VDIM = 2 * DA_HEAD_DIM
DA_WIDTH = DA_HEADS * DA_VDIM
ROPE_THETA = 10000.0
GM_HEADS = 4
GM_HEAD_DIM = 128
GM_WIDTH = GM_HEADS * GM_HEAD_DIM
GM_CHUNK = 128
N_GROUPS = 4
EXPERTS_PER_GROUP = 8
N_EXPERTS = N_GROUPS * EXPERTS_PER_GROUP
TOP_K = 2
EPS = 1e-6

LANES = 128
ROPE_HALF = DA_HEAD_DIM // 2
NEG_BIG = -1e30

ROW_TILE = 512
ATT_TQ = 256
ATT_TK = 512
V_ROWS = DA_VDIM + 16
MOE_ROWS = 512
VMEM_LIMIT = 56 * 1024 * 1024


def _rms(x, g):
    return x * lax.rsqrt(jnp.mean(x * x, axis=-1, keepdims=True) + EPS) * g


def _gelu_tanh(x):
    c = math.sqrt(2.0 / math.pi)
    return x * (0.5 * (1.0 + jnp.tanh(c * (x + 0.044715 * (x * x * x)))))


def _in_proj_body(x_ref, g_ref, w_ref, cos_ref, sin_ref, lng_ref, lnb_ref, ws_ref, bst_ref, og_ref,
                  qt_ref, k_ref, vt_ref, ogm_ref, mix_ref):
    tm = x_ref.shape[0]
    h = _rms(x_ref[...], g_ref[...]).astype(BF16)
    cos = cos_ref[...]
    sin = sin_ref[...]

    zq = jnp.dot(h, w_ref[:, 0:DA_WIDTH], preferred_element_type=F32)
    zk = jnp.dot(h, w_ref[:, DA_WIDTH:2 * DA_WIDTH], preferred_element_type=F32)
    scale = math.log2(math.e) / math.sqrt(DA_HEAD_DIM)
    for hd in range(DA_HEADS):
        sl = slice(hd * LANES, (hd + 1) * LANES)
        qh = zq[:, sl]
        kh = zk[:, sl]
        q_rot = (qh * cos + pltpu.roll(qh, LANES // 2, 1) * sin) * scale
        for c in range(qt_ref.shape[1]):
            qt_ref[hd, c] = q_rot[c * ATT_TQ:(c + 1) * ATT_TQ, :].T.astype(BF16)
        k_ref[:, sl] = (kh * cos + pltpu.roll(kh, LANES // 2, 1) * sin).astype(BF16)

    zv = jnp.dot(h, w_ref[:, 2 * DA_WIDTH:3 * DA_WIDTH], preferred_element_type=F32)
    for hd in range(DA_HEADS):
        vt_ref[hd, 0:DA_VDIM, :] = zv[:, hd * DA_VDIM:(hd + 1) * DA_VDIM].T.astype(BF16)
        vt_ref[hd, DA_VDIM:V_ROWS, :] = jnp.ones((V_ROWS - DA_VDIM, tm), BF16)

    a = _gelu_tanh(jnp.dot(h, w_ref[:, 3 * DA_WIDTH:3 * DA_WIDTH + 2 * GM_WIDTH], preferred_element_type=F32))
    u = a[:, :GM_WIDTH]
    vv = a[:, GM_WIDTH:]
    mu = jnp.mean(vv, axis=-1, keepdims=True)
    vc = vv - mu
    vln = vc * lax.rsqrt(jnp.mean(vc * vc, axis=-1, keepdims=True) + EPS) * lng_ref[...] + lnb_ref[...]
    vb = vln.astype(BF16)
    row = lax.broadcasted_iota(jnp.int32, (GM_CHUNK, GM_CHUNK), 0)
    col = lax.broadcasted_iota(jnp.int32, (GM_CHUNK, GM_CHUNK), 1)
    causal = row >= col
    for hd in range(GM_HEADS):
        wm = jnp.where(causal, ws_ref[hd], 0.0).astype(BF16)
        bias = bst_ref[:, hd:hd + 1]
        cs = slice(hd * GM_HEAD_DIM, (hd + 1) * GM_HEAD_DIM)
        for c in range(tm // GM_CHUNK):
            rs = slice(c * GM_CHUNK, (c + 1) * GM_CHUNK)
            mixed = jnp.dot(wm, vb[rs, cs], preferred_element_type=F32) + bias
            mix_ref[rs, cs] = u[rs, cs] * mixed
    ogm_ref[...] = _rms(mix_ref[...], og_ref[...]).astype(BF16)


def _in_proj(x2d, attn_norm, w_in_b, cos_t, sin_t, ln_g, ln_b, ws, bs_t, out_g, seq):
    t, d = x2d.shape
    tm = ROW_TILE
    assert tm % ATT_TQ == 0 and ATT_TK % tm == 0
    batch = t // seq
    n_pos = seq // tm
    tiles_per_kblk = ATT_TK // tm
    row = lambda i: (i, 0)
    full2 = lambda i: (0, 0)
    return pl.pallas_call(
        _in_proj_body,
        grid=(t // tm,),
        in_specs=[
            pl.BlockSpec((tm, d), row),
            pl.BlockSpec((1, d), full2),
            pl.BlockSpec(w_in_b.shape, full2),
            pl.BlockSpec((tm, LANES), lambda i: (i % n_pos, 0)),
            pl.BlockSpec((tm, LANES), lambda i: (i % n_pos, 0)),
            pl.BlockSpec((1, GM_WIDTH), full2),
            pl.BlockSpec((1, GM_WIDTH), full2),
            pl.BlockSpec(ws.shape, lambda i: (0, 0, 0)),
            pl.BlockSpec(bs_t.shape, full2),
            pl.BlockSpec((1, GM_WIDTH), full2),
        ],
        out_specs=[
            pl.BlockSpec((None, DA_HEADS, tm // ATT_TQ, DA_VDIM, ATT_TQ),
                         lambda i: (i // n_pos, 0, i % n_pos, 0, 0)),
            pl.BlockSpec((tm, DA_WIDTH), row),
            pl.BlockSpec((None, DA_HEADS, None, V_ROWS, tm),
                         lambda i: (i // n_pos, 0, (i % n_pos) // tiles_per_kblk, 0, (i % n_pos) % tiles_per_kblk)),
            pl.BlockSpec((tm, GM_WIDTH), row),
        ],
        out_shape=[
            jax.ShapeDtypeStruct((batch, DA_HEADS, seq // ATT_TQ, DA_VDIM, ATT_TQ), BF16),
            jax.ShapeDtypeStruct((t, DA_WIDTH), BF16),
            jax.ShapeDtypeStruct((batch, DA_HEADS, seq // ATT_TK, V_ROWS, ATT_TK), BF16),
            jax.ShapeDtypeStruct((t, GM_WIDTH), BF16),
        ],
        scratch_shapes=[pltpu.VMEM((tm, GM_WIDTH), F32)],
        compiler_params=pltpu.CompilerParams(dimension_semantics=("parallel",), vmem_limit_bytes=VMEM_LIMIT),
        name="in_proj",
    )(x2d, attn_norm, w_in_b, cos_t, sin_t, ln_g, ln_b, ws, bs_t, out_g)


def _attn_body(lam_ref, g_ref, qt_ref, qtn_ref, k_ref, vt_ref, o_ref, *scratch, tq, tk, lambda_init):
    i = pl.program_id(1)
    nh = qt_ref.shape[0]
    qst_refs, qsn_refs, m_refs, acc_refs, s_refs, bmax_refs = (scratch[n * nh:(n + 1) * nh] for n in range(6))
    cap_ref = scratch[6 * nh]
    caps = cap_ref.shape[0]
    lp = lam_ref[...]
    lam = (jnp.exp(jnp.sum(lp[0:1] * lp[1:2], axis=-1, keepdims=True))
           - jnp.exp(jnp.sum(lp[2:3] * lp[3:4], axis=-1, keepdims=True)) + lambda_init)

    feat = lax.broadcasted_iota(jnp.int32, (DA_VDIM, tq), 0)
    is_map1 = (feat % DA_HEAD_DIM) < ROPE_HALF
    for h in range(nh):
        for src, dst in ((qt_ref, qst_refs), (qtn_ref, qsn_refs)):
            qt = src[h]
            zero = jnp.zeros_like(qt)
            dst[h][:, 0:tq] = jnp.where(is_map1, qt, zero)
            dst[h][:, tq:2 * tq] = jnp.where(is_map1, zero, qt)
        m_refs[h][...] = jnp.full(m_refs[h].shape, NEG_BIG, F32)
        acc_refs[h][...] = jnp.zeros(acc_refs[h].shape, F32)

    def produce_scores(h, q_refs, j):
        row0 = pl.multiple_of(j * tk, tk)
        kj = k_ref[pl.ds(row0, tk), h * LANES:(h + 1) * LANES]
        sc = jnp.dot(kj, q_refs[h][...], preferred_element_type=F32)
        s_refs[h][...] = sc
        bmax_refs[h][...] = jnp.max(sc, axis=0, keepdims=True)

    def step(j, masked):
        for h in range(nh):
            s = s_refs[h][...]
            if masked:
                s = jnp.minimum(s, cap_ref[i % caps])
                bmax = jnp.max(s, axis=0, keepdims=True)
            else:
                bmax = bmax_refs[h][...]
            m_old = m_refs[h][...]
            m_new = jnp.maximum(m_old, bmax)
            alpha = jnp.exp2(m_old - m_new)
            p = jnp.exp2(s - m_new)
            m_refs[h][...] = m_new
            if masked:
                produce_scores(h, qsn_refs, 0)
            else:
                produce_scores(h, qst_refs, j + 1)
            acc_refs[h][...] = (acc_refs[h][...] * alpha
                                + jnp.dot(vt_ref[h, j], p.astype(BF16), preferred_element_type=F32))

    n_full = (i * tq) // tk

    @pl.when(i == 0)
    def _():
        for h in range(nh):
            produce_scores(h, qst_refs, 0)
        r_i = lax.broadcasted_iota(jnp.int32, (tk, 2 * tq), 0)
        c_i = lax.broadcasted_iota(jnp.int32, (tk, 2 * tq), 1) % tq
        for par in range(caps):
            cap_ref[par] = jnp.where(r_i - c_i <= par * tq, -NEG_BIG, NEG_BIG)

    def full_step(j, c):
        step(j, False)
        return c
    lax.fori_loop(0, n_full, full_step, 0)
    step(n_full, True)

    for h in range(nh):
        o = acc_refs[h][0:DA_VDIM, :] * (1.0 / acc_refs[h][DA_VDIM:DA_VDIM + 1, :])
        od = o[:, :tq] - lam * o[:, tq:]
        y = od * lax.rsqrt(jnp.mean(od * od, axis=0, keepdims=True) + EPS) * g_ref[...] * (1.0 - lambda_init)
        o_ref[:, h * LANES:(h + 1) * LANES] = y.T.astype(BF16)


def _attention(lam_params, subln_col, qt5, k2d, vt5, batch, seq, lambda_init):
    tq, tk = ATT_TQ, ATT_TK
    nq, nk = seq // tq, seq // tk
    nh = DA_HEADS
    body = functools.partial(_attn_body, tq=tq, tk=tk, lambda_init=lambda_init)

    def next_q(b, i):
        f = jnp.minimum(b * nq + i + 1, batch * nq - 1)
        return (f // nq, 0, f % nq, 0, 0)
    return pl.pallas_call(
        body,
        grid=(batch, nq),
        in_specs=[
            pl.BlockSpec(lam_params.shape, lambda b, i: (0, 0)),
            pl.BlockSpec(subln_col.shape, lambda b, i: (0, 0)),
            pl.BlockSpec((None, nh, None, DA_VDIM, tq), lambda b, i: (b, 0, i, 0, 0)),
            pl.BlockSpec((None, nh, None, DA_VDIM, tq), next_q),
            pl.BlockSpec((seq, DA_WIDTH), lambda b, i: (b, 0)),
            pl.BlockSpec((None, nh, nk, V_ROWS, tk), lambda b, i: (b, 0, 0, 0, 0)),
        ],
        out_specs=pl.BlockSpec((tq, DA_WIDTH), lambda b, i: (b * nq + i, 0)),
        out_shape=jax.ShapeDtypeStruct((batch * seq, DA_WIDTH), BF16),
        scratch_shapes=(
            [pltpu.VMEM((DA_VDIM, 2 * tq), BF16)] * (2 * nh)
            + [pltpu.VMEM((1, 2 * tq), F32)] * nh
            + [pltpu.VMEM((V_ROWS, 2 * tq), F32)] * nh
            + [pltpu.VMEM((tk, 2 * tq), F32)] * nh
            + [pltpu.VMEM((1, 2 * tq), F32)] * nh
            + [pltpu.VMEM((tk // tq, tk, 2 * tq), F32)]),
        compiler_params=pltpu.CompilerParams(
            dimension_semantics=("arbitrary", "arbitrary"), vmem_limit_bytes=VMEM_LIMIT),
        name="diff_attention",
    )(lam_params, subln_col, qt5, qt5, k2d, vt5)


RT_GATE = 0
RT_EXPERT = 2
RT_RANK = 4
ROUTER_EXPERT_LANE0 = N_GROUPS


TILE_ROWS = 8


def _to_row_tiles(ref, value):
    rows = value.shape[0]
    for c in range(TILE_ROWS):
        ref[pl.ds(c, rows, stride=TILE_ROWS), :] = value[:, c * LANES:(c + 1) * LANES]


def _from_row_tiles(ref, row0, rows):
    return jnp.concatenate(
        [ref[pl.ds(row0 * TILE_ROWS + c, rows, stride=TILE_ROWS), :] for c in range(TILE_ROWS)], axis=1)


def _tile_of(ref, row):
    return ref.at[pl.ds(pl.multiple_of(row * TILE_ROWS, TILE_ROWS), TILE_ROWS)]


def _out_proj_body(x_ref, oda_ref, ogm_ref, w_ref, g_ref, wr_ref, x1_ref, h2_ref, rt_ref, cnt_ref, carry_ref):
    tm = x_ref.shape[0]

    @pl.when(pl.program_id(0) == 0)
    def _():
        carry_ref[...] = jnp.zeros(carry_ref.shape, F32)

    x1 = (x_ref[...]
          + jnp.dot(oda_ref[...], w_ref[0:DA_WIDTH, :], preferred_element_type=F32)
          + jnp.dot(ogm_ref[...], w_ref[DA_WIDTH:DA_WIDTH + GM_WIDTH, :], preferred_element_type=F32))
    x1_ref[...] = x1
    h2 = _rms(x1, g_ref[...])
    _to_row_tiles(h2_ref, h2)
    lg = jnp.dot(h2.astype(BF16), wr_ref[...], preferred_element_type=F32)

    lane = lax.broadcasted_iota(jnp.int32, lg.shape, 1).astype(F32)
    ninf = -jnp.inf
    first_lane = lambda hit: jnp.min(jnp.where(hit, lane, float(LANES)), axis=-1, keepdims=True)
    is_grp = lane < N_GROUPS
    gl = jnp.where(is_grp, lg, ninf)
    gmax = jnp.max(gl, axis=-1, keepdims=True)
    g_idx = first_lane(gl == gmax)
    g_gate = 1.0 / jnp.sum(jnp.where(is_grp, jnp.exp(lg - gmax), 0.0), axis=-1, keepdims=True)
    e_of_lane = lane - ROUTER_EXPERT_LANE0
    e_lo = g_idx * EXPERTS_PER_GROUP
    in_grp = (e_of_lane >= e_lo) & (e_of_lane < e_lo + EXPERTS_PER_GROUP)
    el = jnp.where(in_grp, lg, ninf)
    v1 = jnp.max(el, axis=-1, keepdims=True)
    i1 = first_lane(el == v1)
    el2 = jnp.where(lane == i1, ninf, el)
    v2 = jnp.max(el2, axis=-1, keepdims=True)
    i2 = first_lane(el2 == v2)
    t21 = jnp.exp(v2 - v1)
    w1 = g_gate / (1.0 + t21)
    w2 = g_gate * t21 / (1.0 + t21)

    hit1 = lane == i1
    hit2 = lane == i2
    chosen = (hit1 | hit2).astype(BF16)
    r_i = lax.broadcasted_iota(jnp.int32, (tm, tm), 0)
    c_i = lax.broadcasted_iota(jnp.int32, (tm, tm), 1)
    before = jnp.where(r_i > c_i, 1.0, 0.0).astype(BF16)
    base = carry_ref[...] + jnp.dot(before, chosen, preferred_element_type=F32)
    rank1 = jnp.sum(jnp.where(hit1, base, 0.0), axis=-1, keepdims=True)
    rank2 = jnp.sum(jnp.where(hit2, base, 0.0), axis=-1, keepdims=True)
    carry_ref[...] = carry_ref[...] + jnp.sum(chosen.astype(F32), axis=0, keepdims=True)
    cnt_ref[...] = carry_ref[...]

    rec = jnp.zeros(lg.shape, F32)
    for k, val in ((RT_GATE, w1), (RT_GATE + 1, w2),
                   (RT_EXPERT, (i1 - ROUTER_EXPERT_LANE0).astype(F32)),
                   (RT_EXPERT + 1, (i2 - ROUTER_EXPERT_LANE0).astype(F32)),
                   (RT_RANK, rank1), (RT_RANK + 1, rank2)):
        rec = jnp.where(lane == k, val, rec)
    rt_ref[...] = rec


def _out_proj(x2d, o_da, o_gm, w_out_b, moe_norm, w_router_b):
    t, d = x2d.shape
    tm = ROW_TILE
    row = lambda i: (i, 0)
    full2 = lambda i: (0, 0)
    return pl.pallas_call(
        _out_proj_body,
        grid=(t // tm,),
        in_specs=[
            pl.BlockSpec((tm, d), row),
            pl.BlockSpec((tm, DA_WIDTH), row),
            pl.BlockSpec((tm, GM_WIDTH), row),
            pl.BlockSpec(w_out_b.shape, full2),
            pl.BlockSpec((1, d), full2),
            pl.BlockSpec(w_router_b.shape, full2),
        ],
        out_specs=[
            pl.BlockSpec((tm, d), row),
            pl.BlockSpec((tm * TILE_ROWS, LANES), row),
            pl.BlockSpec((tm, LANES), row),
            pl.BlockSpec((1, LANES), full2),
        ],
        out_shape=[
            jax.ShapeDtypeStruct((t, d), F32),
            jax.ShapeDtypeStruct((t * TILE_ROWS, LANES), F32),
            jax.ShapeDtypeStruct((t, LANES), F32),
            jax.ShapeDtypeStruct((1, LANES), F32),
        ],
        scratch_shapes=[pltpu.VMEM((1, LANES), F32)],
        compiler_params=pltpu.CompilerParams(dimension_semantics=("arbitrary",), vmem_limit_bytes=VMEM_LIMIT),
        name="out_proj",
    )(x2d, o_da, o_gm, w_out_b, moe_norm, w_router_b)


DMA_GROUP = 8


def _wait_copies(make_copy, count):
    def body(r, c):
        make_copy(r).wait()
        return c
    lax.fori_loop(0, count, body, 0, unroll=8)


ZERO_RUN = 32


def _dispatch_body(fill_lo_ref, fill_hi_ref, n_used_ref, dst_ref, h2_ref, buf_hbm, stage, zblk, sem, zsem, rsem,
                   bsem, *, tm, n_blocks):
    i = pl.program_id(0)
    n = pl.num_programs(0)
    slot = i % 2
    stage[slot] = h2_ref[...]

    def row_copy(r, dst, s):
        return pltpu.make_async_copy(_tile_of(stage.at[s], r), _tile_of(buf_hbm, dst), sem.at[s])

    def issue(g, c):
        r0 = g * DMA_GROUP
        dst = [[dst_ref[0, 0, k * tm + r0 + u] for k in range(TOP_K)] for u in range(DMA_GROUP)]
        for u in range(DMA_GROUP):
            for k in range(TOP_K):
                row_copy(r0 + u, dst[u][k], slot).start(priority=k % 2)
        return c
    lax.fori_loop(0, tm // DMA_GROUP, issue, 0)

    def wait_tile(s):
        _wait_copies(lambda r: row_copy(0, 0, s), TOP_K * tm)

    @pl.when(i > 0)
    def _():
        wait_tile(1 - slot)

    @pl.when(i == n - 1)
    def _():
        wait_tile(slot)
        zblk[...] = jnp.zeros(zblk.shape, F32)
        fill = lambda r: pltpu.make_async_copy(zblk.at[pl.ds(0, TILE_ROWS)], _tile_of(buf_hbm, r), zsem)
        run_tiles = ZERO_RUN * TILE_ROWS
        fill_run = lambda r0: pltpu.make_async_copy(
            zblk.at[pl.ds(0, run_tiles)],
            buf_hbm.at[pl.ds(pl.multiple_of(r0 * TILE_ROWS, TILE_ROWS), run_tiles)], rsem)
        blk_tiles = zblk.shape[0]
        fill_blk = lambda b: pltpu.make_async_copy(
            zblk, buf_hbm.at[pl.ds(pl.multiple_of(b * blk_tiles, blk_tiles), blk_tiles)], bsem)

        def do(copy_of, method):
            def body(r, c):
                getattr(copy_of(r), method)()
                return c
            return body
        for method in ("start", "wait"):
            for e in range(N_EXPERTS):
                lo, hi = fill_lo_ref[e], fill_hi_ref[e]
                runs = lax.shift_right_logical(hi - lo, ZERO_RUN.bit_length() - 1)
                lax.fori_loop(0, runs, do(lambda c, hi=hi: fill_run(hi - (c + 1) * ZERO_RUN), method), 0)
                lax.fori_loop(lo, hi - runs * ZERO_RUN, do(fill, method), 0)
            lax.fori_loop(n_used_ref[0], n_blocks, do(fill_blk, method), 0)


def _dispatch(fill_lo, fill_hi, n_used, dest_tiles, h2_tiles, n_rows):
    t = h2_tiles.shape[0] // TILE_ROWS
    tm = ROW_TILE
    grid_spec = pltpu.PrefetchScalarGridSpec(
        num_scalar_prefetch=3,
        grid=(t // tm,),
        in_specs=[
            pl.BlockSpec((1, 1, TOP_K * tm), lambda i, lo, hi, nu: (i, 0, 0), memory_space=pltpu.SMEM),
            pl.BlockSpec((tm * TILE_ROWS, LANES), lambda i, lo, hi, nu: (i, 0)),
        ],
        out_specs=pl.BlockSpec(memory_space=pl.ANY),
        scratch_shapes=[
            pltpu.VMEM((2, tm * TILE_ROWS, LANES), F32),
            pltpu.VMEM((MOE_ROWS * TILE_ROWS, LANES), F32),
            pltpu.SemaphoreType.DMA((2,)),
            pltpu.SemaphoreType.DMA,
            pltpu.SemaphoreType.DMA,
            pltpu.SemaphoreType.DMA,
        ],
    )
    return pl.pallas_call(
        functools.partial(_dispatch_body, tm=tm, n_blocks=n_rows // MOE_ROWS),
        grid_spec=grid_spec,
        out_shape=jax.ShapeDtypeStruct((n_rows * TILE_ROWS, LANES), F32),
        compiler_params=pltpu.CompilerParams(dimension_semantics=("arbitrary",), vmem_limit_bytes=VMEM_LIMIT),
        name="dispatch",
    )(fill_lo, fill_hi, n_used, dest_tiles, h2_tiles)


def _moe_body(blk_exp_ref, x_ref, wg_ref, wu_ref, wd_ref, y_ref, wg_b, wu_b, wd_b):
    i = pl.program_id(0)

    @pl.when((i == 0) | (blk_exp_ref[i] != blk_exp_ref[jnp.maximum(i - 1, 0)]))
    def _():
        wg_b[...] = wg_ref[0].astype(BF16)
        wu_b[...] = wu_ref[0].astype(BF16)
        wd_b[...] = wd_ref[0].astype(BF16)

    xb = _from_row_tiles(x_ref, 0, MOE_ROWS).astype(BF16)
    g = jnp.dot(xb, wg_b[...], preferred_element_type=F32)
    u = jnp.dot(xb, wu_b[...], preferred_element_type=F32)
    act = (g * jax.nn.sigmoid(g) * u).astype(BF16)
    _to_row_tiles(y_ref, jnp.dot(act, wd_b[...], preferred_element_type=F32))


def _moe_ffn(blk_exp, buf, wg, wu, wd):
    n_rows = buf.shape[0] // TILE_ROWS
    d, f = wg.shape[1:]
    rows = MOE_ROWS
    row_blk = lambda i, be: (i, 0)
    w_blk = lambda i, be: (be[i], 0, 0)
    grid_spec = pltpu.PrefetchScalarGridSpec(
        num_scalar_prefetch=1,
        grid=(n_rows // rows,),
        in_specs=[
            pl.BlockSpec((rows * TILE_ROWS, LANES), row_blk),
            pl.BlockSpec((1, d, f), w_blk),
            pl.BlockSpec((1, d, f), w_blk),
            pl.BlockSpec((1, f, d), w_blk),
        ],
        out_specs=pl.BlockSpec((rows * TILE_ROWS, LANES), row_blk),
        scratch_shapes=[pltpu.VMEM((d, f), BF16), pltpu.VMEM((d, f), BF16), pltpu.VMEM((f, d), BF16)],
    )
    return pl.pallas_call(
        _moe_body,
        grid_spec=grid_spec,
        out_shape=jax.ShapeDtypeStruct(buf.shape, F32),
        compiler_params=pltpu.CompilerParams(dimension_semantics=("arbitrary",), vmem_limit_bytes=VMEM_LIMIT),
        name="moe_ffn",
    )(blk_exp, buf, wg, wu, wd)


GATHER_AHEAD = 2


def _finish_body(*refs, last_layer):
    src_refs = refs[:GATHER_AHEAD + 1]
    x1_ref, rt_ref, p_ref, pn_ref, wg_ref, bg_ref, wp_ref, fn_ref, y_hbm, o_ref, ybuf, sem = refs[GATHER_AHEAD + 1:]
    tm = x1_ref.shape[0]
    i = pl.program_id(0)
    n = pl.num_programs(0)
    slots = GATHER_AHEAD + 1
    slot = lax.rem(i, slots)

    def row_copy(src, r, s):
        return pltpu.make_async_copy(_tile_of(y_hbm, src), _tile_of(ybuf.at[s], r), sem.at[s])

    def gather_group(idx_ref, s, r0):
        src = [idx_ref[0, 0, r0 + u] for u in range(TOP_K * DMA_GROUP)]
        for u in range(TOP_K * DMA_GROUP):
            row_copy(src[u], r0 + u, s).start(priority=u % 2)

    @pl.when(i == 0)
    def _():
        for a in range(GATHER_AHEAD):
            def body(g, c, a=a):
                gather_group(src_refs[a], a, g * (TOP_K * DMA_GROUP))
                return c
            lax.fori_loop(0, tm // DMA_GROUP, body, 0)

    _wait_copies(lambda r: row_copy(0, 0, slot), TOP_K * tm)

    ahead_slot = lax.rem(i + GATHER_AHEAD, slots)
    for g in range(tm // DMA_GROUP):
        gather_group(src_refs[GATHER_AHEAD], ahead_slot, g * (TOP_K * DMA_GROUP))

    rt = rt_ref[...]
    y0 = _from_row_tiles(ybuf.at[slot], 0, tm)
    y1 = _from_row_tiles(ybuf.at[slot], tm, tm)
    x2 = x1_ref[...] + rt[:, RT_GATE:RT_GATE + 1] * y0 + rt[:, RT_GATE + 1:RT_GATE + 2] * y1
    hg = _rms(x2, pn_ref[...]).astype(BF16)
    gate = jax.nn.sigmoid(jnp.dot(hg, wg_ref[...], preferred_element_type=F32) + bg_ref[...])
    x3 = x2 + gate * jnp.dot(p_ref[...].astype(BF16), wp_ref[...], preferred_element_type=F32)
    o_ref[...] = _rms(x3, fn_ref[...]) if last_layer else x3

    @pl.when(i == n - 1)
    def _():
        for a in range(1, GATHER_AHEAD + 1):
            _wait_copies(lambda r, a=a: row_copy(0, 0, lax.rem(i + a, slots)), TOP_K * tm)


def _finish(dest_tiles, x1, rt, p2d, ple_norm, w_gate_b, b_gate, w_proj_b, final_norm, y_tiles, last_layer):
    t, d = x1.shape
    tm = ROW_TILE
    n_tiles = t // tm
    row = lambda i: (i, 0)
    full2 = lambda i: (0, 0)
    return pl.pallas_call(
        functools.partial(_finish_body, last_layer=last_layer),
        grid=(n_tiles,),
        in_specs=[
            pl.BlockSpec((1, 1, TOP_K * tm), lambda i, a=a: (jnp.minimum(i + a, n_tiles - 1), 0, 0),
                         memory_space=pltpu.SMEM)
            for a in range(GATHER_AHEAD + 1)
        ] + [
            pl.BlockSpec((tm, d), row),
            pl.BlockSpec((tm, LANES), row),
            pl.BlockSpec((tm, p2d.shape[1]), row),
            pl.BlockSpec((1, d), full2),
            pl.BlockSpec(w_gate_b.shape, full2),
            pl.BlockSpec((1, d), full2),
            pl.BlockSpec(w_proj_b.shape, full2),
            pl.BlockSpec((1, d), full2),
            pl.BlockSpec(memory_space=pl.ANY),
        ],
        out_specs=pl.BlockSpec((tm, d), row),
        out_shape=jax.ShapeDtypeStruct((t, d), F32),
        scratch_shapes=[
            pltpu.VMEM((GATHER_AHEAD + 1, TOP_K * tm * TILE_ROWS, LANES), F32),
            pltpu.SemaphoreType.DMA((GATHER_AHEAD + 1,)),
        ],
        compiler_params=pltpu.CompilerParams(dimension_semantics=("arbitrary",), vmem_limit_bytes=VMEM_LIMIT),
        name="finish",
    )(*([dest_tiles] * (GATHER_AHEAD + 1)), x1, rt, p2d, ple_norm, w_gate_b, b_gate, w_proj_b, final_norm, y_tiles)


def _qk_column_perm():
    perm = np.zeros((DA_WIDTH,), np.int32)
    for h in range(DA_HEADS):
        for half in range(2):
            for mp in range(2):
                for i in range(ROPE_HALF):
                    perm[h * LANES + half * 64 + mp * ROPE_HALF + i] = h * LANES + mp * DA_HEAD_DIM + half * ROPE_HALF + i
    return perm


def _rope_tables(seq):
    inv = 1.0 / (ROPE_THETA ** (jnp.arange(0, DA_HEAD_DIM, 2, dtype=F32) / DA_HEAD_DIM))
    ang = jnp.arange(seq, dtype=F32)[:, None] * inv[None, :]
    cos = jnp.tile(jnp.cos(ang), (1, LANES // ROPE_HALF))
    sin = jnp.tile(jnp.sin(ang), (1, LANES // ROPE_HALF))
    sign = jnp.where(jnp.arange(LANES) < LANES // 2, -1.0, 1.0).astype(F32)
    return cos, sin * sign[None, :]


def _dispatch_tables(rt, counts_f, n_tokens):
    rows = MOE_ROWS
    tm = ROW_TILE
    counts = counts_f[0, ROUTER_EXPERT_LANE0:ROUTER_EXPERT_LANE0 + N_EXPERTS].astype(jnp.int32)
    padded = ((counts + rows - 1) // rows) * rows
    pend = jnp.cumsum(padded)
    pstart = pend - padded
    n_blocks = (n_tokens * TOP_K + N_EXPERTS * rows) // rows
    n_used = (pend[-1:] // rows).astype(jnp.int32)
    blk_start = jnp.arange(n_blocks, dtype=jnp.int32) * rows
    blk_exp = jnp.minimum(jnp.sum(pend[None, :] <= blk_start[:, None], axis=1), N_EXPERTS - 1).astype(jnp.int32)
    expert = rt[:, RT_EXPERT:RT_EXPERT + TOP_K].astype(jnp.int32)
    rank = rt[:, RT_RANK:RT_RANK + TOP_K].astype(jnp.int32)
    first_row = jnp.sum(jnp.where(expert[..., None] == jnp.arange(N_EXPERTS), pstart, 0), axis=-1)
    dest = first_row + rank
    dest_tiles = dest.reshape(n_tokens // tm, tm, TOP_K).transpose(0, 2, 1).reshape(n_tokens // tm, 1, TOP_K * tm)
    return blk_exp, n_used, (pstart + counts).astype(jnp.int32), pend.astype(jnp.int32), dest_tiles, n_blocks * rows


def _layer(x, p_i, i, last_layer, final_norm, attn_norm, w_in, lambda_q1, lambda_k1, lambda_q2, lambda_k2,
           diff_subln, gm_ln_gain, gm_ln_bias, gm_spatial_w, gm_spatial_b, gm_out_norm, w_out, moe_norm,
           w_group_router, w_expert_router, w_expert_gate, w_expert_up, w_expert_down, ple_norm, w_ple_gate,
           b_ple_gate, w_ple_proj):
    batch, seq, d = x.shape
    t = batch * seq
    lambda_init = 0.8 - 0.6 * math.exp(-0.3 * i)
    x2d = x.reshape(t, d)

    perm = _qk_column_perm()
    w_in_b = jnp.concatenate(
        [w_in[:, perm], w_in[:, DA_WIDTH + perm], w_in[:, 2 * DA_WIDTH:]], axis=1).astype(BF16)
    cos_t, sin_t = _rope_tables(seq)
    qt5, k, vt5, o_gm = _in_proj(x2d, attn_norm[None, :], w_in_b, cos_t, sin_t, gm_ln_gain[None, :],
                                 gm_ln_bias[None, :], gm_spatial_w, gm_spatial_b.T, gm_out_norm[None, :], seq)
    lam_params = jnp.stack([lambda_q1, lambda_k1, lambda_q2, lambda_k2]).astype(F32)
    o_da = _attention(lam_params, diff_subln[:, None], qt5, k, vt5, batch, seq, lambda_init)

    w_router = jnp.concatenate([w_group_router, w_expert_router], axis=1)
    w_router_b = jnp.pad(w_router, ((0, 0), (0, LANES - w_router.shape[1]))).astype(BF16)
    x1, h2_tiles, rt, counts_f = _out_proj(x2d, o_da, o_gm, w_out.astype(BF16), moe_norm[None, :], w_router_b)

    blk_exp, n_used, fill_lo, fill_hi, dest_tiles, n_rows = _dispatch_tables(rt, counts_f, t)
    buf = _dispatch(fill_lo, fill_hi, n_used, dest_tiles, h2_tiles, n_rows)
    y_tiles = _moe_ffn(blk_exp, buf, w_expert_gate, w_expert_up, w_expert_down)

    out = _finish(dest_tiles, x1, rt, p_i.reshape(t, -1), ple_norm[None, :], w_ple_gate.astype(BF16),
                  b_ple_gate[None, :], w_ple_proj.astype(BF16), final_norm[None, :], y_tiles, last_layer)
    return out.reshape(batch, seq, d)


def kernel(x, p, attn_norm, w_in, lambda_q1, lambda_k1, lambda_q2, lambda_k2, diff_subln, gm_ln_gain, gm_ln_bias, gm_spatial_w, gm_spatial_b, gm_out_norm, w_out, moe_norm, w_group_router, w_expert_router, w_expert_gate, w_expert_up, w_expert_down, ple_norm, w_ple_gate, b_ple_gate, w_ple_proj, final_norm):
    per_layer = (attn_norm, w_in, lambda_q1, lambda_k1, lambda_q2, lambda_k2, diff_subln, gm_ln_gain, gm_ln_bias,
                 gm_spatial_w, gm_spatial_b, gm_out_norm, w_out, moe_norm, w_group_router, w_expert_router,
                 w_expert_gate, w_expert_up, w_expert_down, ple_norm, w_ple_gate, b_ple_gate, w_ple_proj)
    depth = attn_norm.shape[0]
    for i in range(depth):
        x = _layer(x, p[i], i, i == depth - 1, final_norm, *(w[i] for w in per_layer))
    return x
```

```python
import functools
import math

import numpy as np
import jax
import jax.numpy as jnp
from jax import lax
from jax.experimental import pallas as pl
from jax.experimental.pallas import tpu as pltpu

F32 = jnp.float32
BF16 = jnp.bfloat16

DA_HEADS = 4
DA_HEAD_DIM = 64
DA_VDIM = 2 * DA_HEAD_DIM
DA_WIDTH = DA_HEADS * DA_VDIM
ROPE_THETA = 10000.0
GM_HEADS = 4
GM_HEAD_DIM = 128
GM_WIDTH = GM_HEADS * GM_HEAD_DIM
GM_CHUNK = 128
N_GROUPS = 4
EXPERTS_PER_GROUP = 8
N_EXPERTS = N_GROUPS * EXPERTS_PER_GROUP
TOP_K = 2
EPS = 1e-6

LANES = 128
ROPE_HALF = DA_HEAD_DIM // 2
NEG_BIG = -1e30

ROW_TILE = 512
ATT_TQ = 256
ATT_TK = 512
V_ROWS = DA_VDIM + 16
MOE_ROWS = 512
VMEM_LIMIT = 56 * 1024 * 1024


def _rms(x, g):
    return x * lax.rsqrt(jnp.mean(x * x, axis=-1, keepdims=True) + EPS) * g


def _gelu_tanh(x):
    c = math.sqrt(2.0 / math.pi)
    return x * (0.5 * (1.0 + jnp.tanh(c * (x + 0.044715 * (x * x * x)))))


def _in_proj_body(x_ref, g_ref, w_ref, cos_ref, sin_ref, lng_ref, lnb_ref, ws_ref, bst_ref, og_ref,
                  qt_ref, k_ref, vt_ref, ogm_ref, mix_ref):
    tm = x_ref.shape[0]
    h = _rms(x_ref[...], g_ref[...]).astype(BF16)
    cos = cos_ref[...]
    sin = sin_ref[...]

    zq = jnp.dot(h, w_ref[:, 0:DA_WIDTH], preferred_element_type=F32)
    zk = jnp.dot(h, w_ref[:, DA_WIDTH:2 * DA_WIDTH], preferred_element_type=F32)
    scale = math.log2(math.e) / math.sqrt(DA_HEAD_DIM)
    for hd in range(DA_HEADS):
        sl = slice(hd * LANES, (hd + 1) * LANES)
        qh = zq[:, sl]
        kh = zk[:, sl]
        q_rot = (qh * cos + pltpu.roll(qh, LANES // 2, 1) * sin) * scale
        for c in range(qt_ref.shape[1]):
            qt_ref[hd, c] = q_rot[c * ATT_TQ:(c + 1) * ATT_TQ, :].T.astype(BF16)
        k_ref[:, sl] = (kh * cos + pltpu.roll(kh, LANES // 2, 1) * sin).astype(BF16)

    zv = jnp.dot(h, w_ref[:, 2 * DA_WIDTH:3 * DA_WIDTH], preferred_element_type=F32)
    for hd in range(DA_HEADS):
        vt_ref[hd, 0:DA_VDIM, :] = zv[:, hd * DA_VDIM:(hd + 1) * DA_VDIM].T.astype(BF16)
        vt_ref[hd, DA_VDIM:V_ROWS, :] = jnp.ones((V_ROWS - DA_VDIM, tm), BF16)

    a = _gelu_tanh(jnp.dot(h, w_ref[:, 3 * DA_WIDTH:3 * DA_WIDTH + 2 * GM_WIDTH], preferred_element_type=F32))
    u = a[:, :GM_WIDTH]
    vv = a[:, GM_WIDTH:]
    mu = jnp.mean(vv, axis=-1, keepdims=True)
    vc = vv - mu
    vln = vc * lax.rsqrt(jnp.mean(vc * vc, axis=-1, keepdims=True) + EPS) * lng_ref[...] + lnb_ref[...]
    vb = vln.astype(BF16)
    row = lax.broadcasted_iota(jnp.int32, (GM_CHUNK, GM_CHUNK), 0)
    col = lax.broadcasted_iota(jnp.int32, (GM_CHUNK, GM_CHUNK), 1)
    causal = row >= col
    for hd in range(GM_HEADS):
        wm = jnp.where(causal, ws_ref[hd], 0.0).astype(BF16)
        bias = bst_ref[:, hd:hd + 1]
        cs = slice(hd * GM_HEAD_DIM, (hd + 1) * GM_HEAD_DIM)
        for c in range(tm // GM_CHUNK):
            rs = slice(c * GM_CHUNK, (c + 1) * GM_CHUNK)
            mixed = jnp.dot(wm, vb[rs, cs], preferred_element_type=F32) + bias
            mix_ref[rs, cs] = u[rs, cs] * mixed
    ogm_ref[...] = _rms(mix_ref[...], og_ref[...]).astype(BF16)


def _in_proj(x2d, attn_norm, w_in_b, cos_t, sin_t, ln_g, ln_b, ws, bs_t, out_g, seq):
    t, d = x2d.shape
    tm = ROW_TILE
    assert tm % ATT_TQ == 0 and ATT_TK % tm == 0
    batch = t // seq
    n_pos = seq // tm
    tiles_per_kblk = ATT_TK // tm
    row = lambda i: (i, 0)
    full2 = lambda i: (0, 0)
    return pl.pallas_call(
        _in_proj_body,
        grid=(t // tm,),
        in_specs=[
            pl.BlockSpec((tm, d), row),
            pl.BlockSpec((1, d), full2),
            pl.BlockSpec(w_in_b.shape, full2),
            pl.BlockSpec((tm, LANES), lambda i: (i % n_pos, 0)),
            pl.BlockSpec((tm, LANES), lambda i: (i % n_pos, 0)),
            pl.BlockSpec((1, GM_WIDTH), full2),
            pl.BlockSpec((1, GM_WIDTH), full2),
            pl.BlockSpec(ws.shape, lambda i: (0, 0, 0)),
            pl.BlockSpec(bs_t.shape, full2),
            pl.BlockSpec((1, GM_WIDTH), full2),
        ],
        out_specs=[
            pl.BlockSpec((None, DA_HEADS, tm // ATT_TQ, DA_VDIM, ATT_TQ),
                         lambda i: (i // n_pos, 0, i % n_pos, 0, 0)),
            pl.BlockSpec((tm, DA_WIDTH), row),
            pl.BlockSpec((None, DA_HEADS, None, V_ROWS, tm),
                         lambda i: (i // n_pos, 0, (i % n_pos) // tiles_per_kblk, 0, (i % n_pos) % tiles_per_kblk)),
            pl.BlockSpec((tm, GM_WIDTH), row),
        ],
        out_shape=[
            jax.ShapeDtypeStruct((batch, DA_HEADS, seq // ATT_TQ, DA_VDIM, ATT_TQ), BF16),
            jax.ShapeDtypeStruct((t, DA_WIDTH), BF16),
            jax.ShapeDtypeStruct((batch, DA_HEADS, seq // ATT_TK, V_ROWS, ATT_TK), BF16),
            jax.ShapeDtypeStruct((t, GM_WIDTH), BF16),
        ],
        scratch_shapes=[pltpu.VMEM((tm, GM_WIDTH), F32)],
        compiler_params=pltpu.CompilerParams(dimension_semantics=("parallel",), vmem_limit_bytes=VMEM_LIMIT),
        name="in_proj",
    )(x2d, attn_norm, w_in_b, cos_t, sin_t, ln_g, ln_b, ws, bs_t, out_g)


def _attn_body(lam_ref, g_ref, qt_ref, qtn_ref, k_ref, vt_ref, o_ref, *scratch, tq, tk, lambda_init):
    i = pl.program_id(1)
    nh = qt_ref.shape[0]
    qst_refs, qsn_refs, m_refs, acc_refs, s_refs, bmax_refs = (scratch[n * nh:(n + 1) * nh] for n in range(6))
    cap_ref = scratch[6 * nh]
    caps = cap_ref.shape[0]
    lp = lam_ref[...]
    lam = (jnp.exp(jnp.sum(lp[0:1] * lp[1:2], axis=-1, keepdims=True))
           - jnp.exp(jnp.sum(lp[2:3] * lp[3:4], axis=-1, keepdims=True)) + lambda_init)

    feat = lax.broadcasted_iota(jnp.int32, (DA_VDIM, tq), 0)
    is_map1 = (feat % DA_HEAD_DIM) < ROPE_HALF
    for h in range(nh):
        for src, dst in ((qt_ref, qst_refs), (qtn_ref, qsn_refs)):
            qt = src[h]
            zero = jnp.zeros_like(qt)
            dst[h][:, 0:tq] = jnp.where(is_map1, qt, zero)
            dst[h][:, tq:2 * tq] = jnp.where(is_map1, zero, qt)
        m_refs[h][...] = jnp.full(m_refs[h].shape, NEG_BIG, F32)
        acc_refs[h][...] = jnp.zeros(acc_refs[h].shape, F32)

    def produce_scores(h, q_refs, j):
        row0 = pl.multiple_of(j * tk, tk)
        kj = k_ref[pl.ds(row0, tk), h * LANES:(h + 1) * LANES]
        sc = jnp.dot(kj, q_refs[h][...], preferred_element_type=F32)
        s_refs[h][...] = sc
        bmax_refs[h][...] = jnp.max(sc, axis=0, keepdims=True)

    def step(j, masked):
        for h in range(nh):
            s = s_refs[h][...]
            if masked:
                s = jnp.minimum(s, cap_ref[i % caps])
                bmax = jnp.max(s, axis=0, keepdims=True)
            else:
                bmax = bmax_refs[h][...]
            m_old = m_refs[h][...]
            m_new = jnp.maximum(m_old, bmax)
            alpha = jnp.exp2(m_old - m_new)
            p = jnp.exp2(s - m_new)
            m_refs[h][...] = m_new
            if masked:
                produce_scores(h, qsn_refs, 0)
            else:
                produce_scores(h, qst_refs, j + 1)
            acc_refs[h][...] = (acc_refs[h][...] * alpha
                                + jnp.dot(vt_ref[h, j], p.astype(BF16), preferred_element_type=F32))

    n_full = (i * tq) // tk

    @pl.when(i == 0)
    def _():
        for h in range(nh):
            produce_scores(h, qst_refs, 0)
        r_i = lax.broadcasted_iota(jnp.int32, (tk, 2 * tq), 0)
        c_i = lax.broadcasted_iota(jnp.int32, (tk, 2 * tq), 1) % tq
        for par in range(caps):
            cap_ref[par] = jnp.where(r_i - c_i <= par * tq, -NEG_BIG, NEG_BIG)

    def full_step(j, c):
        step(j, False)
        return c
    lax.fori_loop(0, n_full, full_step, 0)
    step(n_full, True)

    for h in range(nh):
        o = acc_refs[h][0:DA_VDIM, :] * (1.0 / acc_refs[h][DA_VDIM:DA_VDIM + 1, :])
        od = o[:, :tq] - lam * o[:, tq:]
        y = od * lax.rsqrt(jnp.mean(od * od, axis=0, keepdims=True) + EPS) * g_ref[...] * (1.0 - lambda_init)
        o_ref[:, h * LANES:(h + 1) * LANES] = y.T.astype(BF16)


def _attention(lam_params, subln_col, qt5, k2d, vt5, batch, seq, lambda_init):
    tq, tk = ATT_TQ, ATT_TK
    nq, nk = seq // tq, seq // tk
    nh = DA_HEADS
    body = functools.partial(_attn_body, tq=tq, tk=tk, lambda_init=lambda_init)

    def next_q(b, i):
        f = jnp.minimum(b * nq + i + 1, batch * nq - 1)
        return (f // nq, 0, f % nq, 0, 0)
    return pl.pallas_call(
        body,
        grid=(batch, nq),
        in_specs=[
            pl.BlockSpec(lam_params.shape, lambda b, i: (0, 0)),
            pl.BlockSpec(subln_col.shape, lambda b, i: (0, 0)),
            pl.BlockSpec((None, nh, None, DA_VDIM, tq), lambda b, i: (b, 0, i, 0, 0)),
            pl.BlockSpec((None, nh, None, DA_VDIM, tq), next_q),
            pl.BlockSpec((seq, DA_WIDTH), lambda b, i: (b, 0)),
            pl.BlockSpec((None, nh, nk, V_ROWS, tk), lambda b, i: (b, 0, 0, 0, 0)),
        ],
        out_specs=pl.BlockSpec((tq, DA_WIDTH), lambda b, i: (b * nq + i, 0)),
        out_shape=jax.ShapeDtypeStruct((batch * seq, DA_WIDTH), BF16),
        scratch_shapes=(
            [pltpu.VMEM((DA_VDIM, 2 * tq), BF16)] * (2 * nh)
            + [pltpu.VMEM((1, 2 * tq), F32)] * nh
            + [pltpu.VMEM((V_ROWS, 2 * tq), F32)] * nh
            + [pltpu.VMEM((tk, 2 * tq), F32)] * nh
            + [pltpu.VMEM((1, 2 * tq), F32)] * nh
            + [pltpu.VMEM((tk // tq, tk, 2 * tq), F32)]),
        compiler_params=pltpu.CompilerParams(
            dimension_semantics=("arbitrary", "arbitrary"), vmem_limit_bytes=VMEM_LIMIT),
        name="diff_attention",
    )(lam_params, subln_col, qt5, qt5, k2d, vt5)


RT_GATE = 0
RT_EXPERT = 2
RT_RANK = 4
ROUTER_EXPERT_LANE0 = N_GROUPS


TILE_ROWS = 8


def _to_row_tiles(ref, value):
    rows = value.shape[0]
    for c in range(TILE_ROWS):
        ref[pl.ds(c, rows, stride=TILE_ROWS), :] = value[:, c * LANES:(c + 1) * LANES]


def _from_row_tiles(ref, row0, rows):
    return jnp.concatenate(
        [ref[pl.ds(row0 * TILE_ROWS + c, rows, stride=TILE_ROWS), :] for c in range(TILE_ROWS)], axis=1)


def _tile_of(ref, row):
    return ref.at[pl.ds(pl.multiple_of(row * TILE_ROWS, TILE_ROWS), TILE_ROWS)]


def _out_proj_body(x_ref, oda_ref, ogm_ref, w_ref, g_ref, wr_ref, x1_ref, h2_ref, rt_ref, cnt_ref, carry_ref):
    tm = x_ref.shape[0]

    @pl.when(pl.program_id(0) == 0)
    def _():
        carry_ref[...] = jnp.zeros(carry_ref.shape, F32)

    x1 = (x_ref[...]
          + jnp.dot(oda_ref[...], w_ref[0:DA_WIDTH, :], preferred_element_type=F32)
          + jnp.dot(ogm_ref[...], w_ref[DA_WIDTH:DA_WIDTH + GM_WIDTH, :], preferred_element_type=F32))
    x1_ref[...] = x1
    h2 = _rms(x1, g_ref[...])
    _to_row_tiles(h2_ref, h2)
    lg = jnp.dot(h2.astype(BF16), wr_ref[...], preferred_element_type=F32)

    lane = lax.broadcasted_iota(jnp.int32, lg.shape, 1).astype(F32)
    ninf = -jnp.inf
    first_lane = lambda hit: jnp.min(jnp.where(hit, lane, float(LANES)), axis=-1, keepdims=True)
    is_grp = lane < N_GROUPS
    gl = jnp.where(is_grp, lg, ninf)
    gmax = jnp.max(gl, axis=-1, keepdims=True)
    g_idx = first_lane(gl == gmax)
    g_gate = 1.0 / jnp.sum(jnp.where(is_grp, jnp.exp(lg - gmax), 0.0), axis=-1, keepdims=True)
    e_of_lane = lane - ROUTER_EXPERT_LANE0
    e_lo = g_idx * EXPERTS_PER_GROUP
    in_grp = (e_of_lane >= e_lo) & (e_of_lane < e_lo + EXPERTS_PER_GROUP)
    el = jnp.where(in_grp, lg, ninf)
    v1 = jnp.max(el, axis=-1, keepdims=True)
    i1 = first_lane(el == v1)
    el2 = jnp.where(lane == i1, ninf, el)
    v2 = jnp.max(el2, axis=-1, keepdims=True)
    i2 = first_lane(el2 == v2)
    t21 = jnp.exp(v2 - v1)
    w1 = g_gate / (1.0 + t21)
    w2 = g_gate * t21 / (1.0 + t21)

    hit1 = lane == i1
    hit2 = lane == i2
    chosen = (hit1 | hit2).astype(BF16)
    r_i = lax.broadcasted_iota(jnp.int32, (tm, tm), 0)
    c_i = lax.broadcasted_iota(jnp.int32, (tm, tm), 1)
    before = jnp.where(r_i > c_i, 1.0, 0.0).astype(BF16)
    base = carry_ref[...] + jnp.dot(before, chosen, preferred_element_type=F32)
    rank1 = jnp.sum(jnp.where(hit1, base, 0.0), axis=-1, keepdims=True)
    rank2 = jnp.sum(jnp.where(hit2, base, 0.0), axis=-1, keepdims=True)
    carry_ref[...] = carry_ref[...] + jnp.sum(chosen.astype(F32), axis=0, keepdims=True)
    cnt_ref[...] = carry_ref[...]

    rec = jnp.zeros(lg.shape, F32)
    for k, val in ((RT_GATE, w1), (RT_GATE + 1, w2),
                   (RT_EXPERT, (i1 - ROUTER_EXPERT_LANE0).astype(F32)),
                   (RT_EXPERT + 1, (i2 - ROUTER_EXPERT_LANE0).astype(F32)),
                   (RT_RANK, rank1), (RT_RANK + 1, rank2)):
        rec = jnp.where(lane == k, val, rec)
    rt_ref[...] = rec


def _out_proj(x2d, o_da, o_gm, w_out_b, moe_norm, w_router_b):
    t, d = x2d.shape
    tm = ROW_TILE
    row = lambda i: (i, 0)
    full2 = lambda i: (0, 0)
    return pl.pallas_call(
        _out_proj_body,
        grid=(t // tm,),
        in_specs=[
            pl.BlockSpec((tm, d), row),
            pl.BlockSpec((tm, DA_WIDTH), row),
            pl.BlockSpec((tm, GM_WIDTH), row),
            pl.BlockSpec(w_out_b.shape, full2),
            pl.BlockSpec((1, d), full2),
            pl.BlockSpec(w_router_b.shape, full2),
        ],
        out_specs=[
            pl.BlockSpec((tm, d), row),
            pl.BlockSpec((tm * TILE_ROWS, LANES), row),
            pl.BlockSpec((tm, LANES), row),
            pl.BlockSpec((1, LANES), full2),
        ],
        out_shape=[
            jax.ShapeDtypeStruct((t, d), F32),
            jax.ShapeDtypeStruct((t * TILE_ROWS, LANES), F32),
            jax.ShapeDtypeStruct((t, LANES), F32),
            jax.ShapeDtypeStruct((1, LANES), F32),
        ],
        scratch_shapes=[pltpu.VMEM((1, LANES), F32)],
        compiler_params=pltpu.CompilerParams(dimension_semantics=("arbitrary",), vmem_limit_bytes=VMEM_LIMIT),
        name="out_proj",
    )(x2d, o_da, o_gm, w_out_b, moe_norm, w_router_b)


DMA_GROUP = 8


ZERO_RUN = 32


def _dispatch_body(fill_lo_ref, fill_hi_ref, n_used_ref, dst_ref, h2_ref, buf_hbm, stage, zblk, sem, zsem, rsem,
                   bsem, *, tm, n_blocks):
    i = pl.program_id(0)
    n = pl.num_programs(0)
    slot = i % 2
    stage[slot] = h2_ref[...]

    def row_copy(r, dst, s):
        return pltpu.make_async_copy(_tile_of(stage.at[s], r), _tile_of(buf_hbm, dst), sem.at[s])

    def issue(g, c):
        r0 = g * DMA_GROUP
        dst = [[dst_ref[0, 0, k * tm + r0 + u] for k in range(TOP_K)] for u in range(DMA_GROUP)]
        for u in range(DMA_GROUP):
            for k in range(TOP_K):
                row_copy(r0 + u, dst[u][k], slot).start(priority=k % 2)
        return c
    lax.fori_loop(0, tm // DMA_GROUP, issue, 0)

    def wait_tile(s):
        whole = pltpu.make_async_copy(stage.at[s], buf_hbm.at[pl.ds(0, stage.shape[1])], sem.at[s])
        for _ in range(TOP_K):
            whole.wait()

    @pl.when(i > 0)
    def _():
        wait_tile(1 - slot)

    @pl.when(i == n - 1)
    def _():
        wait_tile(slot)
        zblk[...] = jnp.zeros(zblk.shape, F32)
        fill = lambda r: pltpu.make_async_copy(zblk.at[pl.ds(0, TILE_ROWS)], _tile_of(buf_hbm, r), zsem)
        run_tiles = ZERO_RUN * TILE_ROWS
        fill_run = lambda r0: pltpu.make_async_copy(
            zblk.at[pl.ds(0, run_tiles)],
            buf_hbm.at[pl.ds(pl.multiple_of(r0 * TILE_ROWS, TILE_ROWS), run_tiles)], rsem)
        blk_tiles = zblk.shape[0]
        fill_blk = lambda b: pltpu.make_async_copy(
            zblk, buf_hbm.at[pl.ds(pl.multiple_of(b * blk_tiles, blk_tiles), blk_tiles)], bsem)

        def do(copy_of, method):
            def body(r, c):
                getattr(copy_of(r), method)()
                return c
            return body
        for method in ("start", "wait"):
            for e in range(N_EXPERTS):
                lo, hi = fill_lo_ref[e], fill_hi_ref[e]
                runs = lax.shift_right_logical(hi - lo, ZERO_RUN.bit_length() - 1)
                lax.fori_loop(0, runs, do(lambda c, hi=hi: fill_run(hi - (c + 1) * ZERO_RUN), method), 0)
                lax.fori_loop(lo, hi - runs * ZERO_RUN, do(fill, method), 0)
            lax.fori_loop(n_used_ref[0], n_blocks, do(fill_blk, method), 0)


def _dispatch(fill_lo, fill_hi, n_used, dest_tiles, h2_tiles, n_rows):
    t = h2_tiles.shape[0] // TILE_ROWS
    tm = ROW_TILE
    grid_spec = pltpu.PrefetchScalarGridSpec(
        num_scalar_prefetch=3,
        grid=(t // tm,),
        in_specs=[
            pl.BlockSpec((1, 1, TOP_K * tm), lambda i, lo, hi, nu: (i, 0, 0), memory_space=pltpu.SMEM),
            pl.BlockSpec((tm * TILE_ROWS, LANES), lambda i, lo, hi, nu: (i, 0)),
        ],
        out_specs=pl.BlockSpec(memory_space=pl.ANY),
        scratch_shapes=[
            pltpu.VMEM((2, tm * TILE_ROWS, LANES), F32),
            pltpu.VMEM((MOE_ROWS * TILE_ROWS, LANES), F32),
            pltpu.SemaphoreType.DMA((2,)),
            pltpu.SemaphoreType.DMA,
            pltpu.SemaphoreType.DMA,
            pltpu.SemaphoreType.DMA,
        ],
    )
    return pl.pallas_call(
        functools.partial(_dispatch_body, tm=tm, n_blocks=n_rows // MOE_ROWS),
        grid_spec=grid_spec,
        out_shape=jax.ShapeDtypeStruct((n_rows * TILE_ROWS, LANES), F32),
        compiler_params=pltpu.CompilerParams(dimension_semantics=("arbitrary",), vmem_limit_bytes=VMEM_LIMIT),
        name="dispatch",
    )(fill_lo, fill_hi, n_used, dest_tiles, h2_tiles)


def _moe_body(blk_exp_ref, x_ref, wg_ref, wu_ref, wd_ref, y_ref, wg_b, wu_b, wd_b):
    i = pl.program_id(0)

    @pl.when((i == 0) | (blk_exp_ref[i] != blk_exp_ref[jnp.maximum(i - 1, 0)]))
    def _():
        wg_b[...] = wg_ref[0].astype(BF16)
        wu_b[...] = wu_ref[0].astype(BF16)
        wd_b[...] = wd_ref[0].astype(BF16)

    xb = _from_row_tiles(x_ref, 0, MOE_ROWS).astype(BF16)
    g = jnp.dot(xb, wg_b[...], preferred_element_type=F32)
    u = jnp.dot(xb, wu_b[...], preferred_element_type=F32)
    act = (g * jax.nn.sigmoid(g) * u).astype(BF16)
    _to_row_tiles(y_ref, jnp.dot(act, wd_b[...], preferred_element_type=F32))


def _moe_ffn(blk_exp, buf, wg, wu, wd):
    n_rows = buf.shape[0] // TILE_ROWS
    d, f = wg.shape[1:]
    rows = MOE_ROWS
    row_blk = lambda i, be: (i, 0)
    w_blk = lambda i, be: (be[i], 0, 0)
    grid_spec = pltpu.PrefetchScalarGridSpec(
        num_scalar_prefetch=1,
        grid=(n_rows // rows,),
        in_specs=[
            pl.BlockSpec((rows * TILE_ROWS, LANES), row_blk),
            pl.BlockSpec((1, d, f), w_blk),
            pl.BlockSpec((1, d, f), w_blk),
            pl.BlockSpec((1, f, d), w_blk),
        ],
        out_specs=pl.BlockSpec((rows * TILE_ROWS, LANES), row_blk),
        scratch_shapes=[pltpu.VMEM((d, f), BF16), pltpu.VMEM((d, f), BF16), pltpu.VMEM((f, d), BF16)],
    )
    return pl.pallas_call(
        _moe_body,
        grid_spec=grid_spec,
        out_shape=jax.ShapeDtypeStruct(buf.shape, F32),
        compiler_params=pltpu.CompilerParams(dimension_semantics=("arbitrary",), vmem_limit_bytes=VMEM_LIMIT),
        name="moe_ffn",
    )(blk_exp, buf, wg, wu, wd)


GATHER_AHEAD = 2


def _finish_body(*refs, last_layer):
    src_refs = refs[:GATHER_AHEAD + 1]
    x1_ref, rt_ref, p_ref, pn_ref, wg_ref, bg_ref, wp_ref, fn_ref, y_hbm, o_ref, ybuf, sem = refs[GATHER_AHEAD + 1:]
    tm = x1_ref.shape[0]
    i = pl.program_id(0)
    n = pl.num_programs(0)
    slots = GATHER_AHEAD + 1
    slot = lax.rem(i, slots)

    def row_copy(src, r, s):
        return pltpu.make_async_copy(_tile_of(y_hbm, src), _tile_of(ybuf.at[s], r), sem.at[s])

    def gather_group(idx_ref, s, r0):
        src = [idx_ref[0, 0, r0 + u] for u in range(TOP_K * DMA_GROUP)]
        for u in range(TOP_K * DMA_GROUP):
            row_copy(src[u], r0 + u, s).start(priority=u % 2)

    @pl.when(i == 0)
    def _():
        for a in range(GATHER_AHEAD):
            def body(g, c, a=a):
                gather_group(src_refs[a], a, g * (TOP_K * DMA_GROUP))
                return c
            lax.fori_loop(0, tm // DMA_GROUP, body, 0)

    def wait_slot(s):
        pltpu.make_async_copy(y_hbm.at[pl.ds(0, ybuf.shape[1])], ybuf.at[s], sem.at[s]).wait()

    wait_slot(slot)

    ahead_slot = lax.rem(i + GATHER_AHEAD, slots)
    for g in range(tm // DMA_GROUP):
        gather_group(src_refs[GATHER_AHEAD], ahead_slot, g * (TOP_K * DMA_GROUP))

    rt = rt_ref[...]
    y0 = _from_row_tiles(ybuf.at[slot], 0, tm)
    y1 = _from_row_tiles(ybuf.at[slot], tm, tm)
    x2 = x1_ref[...] + rt[:, RT_GATE:RT_GATE + 1] * y0 + rt[:, RT_GATE + 1:RT_GATE + 2] * y1
    hg = _rms(x2, pn_ref[...]).astype(BF16)
    gate = jax.nn.sigmoid(jnp.dot(hg, wg_ref[...], preferred_element_type=F32) + bg_ref[...])
    x3 = x2 + gate * jnp.dot(p_ref[...].astype(BF16), wp_ref[...], preferred_element_type=F32)
    o_ref[...] = _rms(x3, fn_ref[...]) if last_layer else x3

    @pl.when(i == n - 1)
    def _():
        for a in range(1, GATHER_AHEAD + 1):
            wait_slot(lax.rem(i + a, slots))


def _finish(dest_tiles, x1, rt, p2d, ple_norm, w_gate_b, b_gate, w_proj_b, final_norm, y_tiles, last_layer):
    t, d = x1.shape
    tm = ROW_TILE
    n_tiles = t // tm
    row = lambda i: (i, 0)
    full2 = lambda i: (0, 0)
    return pl.pallas_call(
        functools.partial(_finish_body, last_layer=last_layer),
        grid=(n_tiles,),
        in_specs=[
            pl.BlockSpec((1, 1, TOP_K * tm), lambda i, a=a: (jnp.minimum(i + a, n_tiles - 1), 0, 0),
                         memory_space=pltpu.SMEM)
            for a in range(GATHER_AHEAD + 1)
        ] + [
            pl.BlockSpec((tm, d), row),
            pl.BlockSpec((tm, LANES), row),
            pl.BlockSpec((tm, p2d.shape[1]), row),
            pl.BlockSpec((1, d), full2),
            pl.BlockSpec(w_gate_b.shape, full2),
            pl.BlockSpec((1, d), full2),
            pl.BlockSpec(w_proj_b.shape, full2),
            pl.BlockSpec((1, d), full2),
            pl.BlockSpec(memory_space=pl.ANY),
        ],
        out_specs=pl.BlockSpec((tm, d), row),
        out_shape=jax.ShapeDtypeStruct((t, d), F32),
        scratch_shapes=[
            pltpu.VMEM((GATHER_AHEAD + 1, TOP_K * tm * TILE_ROWS, LANES), F32),
            pltpu.SemaphoreType.DMA((GATHER_AHEAD + 1,)),
        ],
        compiler_params=pltpu.CompilerParams(dimension_semantics=("arbitrary",), vmem_limit_bytes=VMEM_LIMIT),
        name="finish",
    )(*([dest_tiles] * (GATHER_AHEAD + 1)), x1, rt, p2d, ple_norm, w_gate_b, b_gate, w_proj_b, final_norm, y_tiles)


def _qk_column_perm():
    perm = np.zeros((DA_WIDTH,), np.int32)
    for h in range(DA_HEADS):
        for half in range(2):
            for mp in range(2):
                for i in range(ROPE_HALF):
                    perm[h * LANES + half * 64 + mp * ROPE_HALF + i] = h * LANES + mp * DA_HEAD_DIM + half * ROPE_HALF + i
    return perm


def _rope_tables(seq):
    inv = 1.0 / (ROPE_THETA ** (jnp.arange(0, DA_HEAD_DIM, 2, dtype=F32) / DA_HEAD_DIM))
    ang = jnp.arange(seq, dtype=F32)[:, None] * inv[None, :]
    cos = jnp.tile(jnp.cos(ang), (1, LANES // ROPE_HALF))
    sin = jnp.tile(jnp.sin(ang), (1, LANES // ROPE_HALF))
    sign = jnp.where(jnp.arange(LANES) < LANES // 2, -1.0, 1.0).astype(F32)
    return cos, sin * sign[None, :]


def _dispatch_tables(rt, counts_f, n_tokens):
    rows = MOE_ROWS
    tm = ROW_TILE
    counts = counts_f[0, ROUTER_EXPERT_LANE0:ROUTER_EXPERT_LANE0 + N_EXPERTS].astype(jnp.int32)
    padded = ((counts + rows - 1) // rows) * rows
    pend = jnp.cumsum(padded)
    pstart = pend - padded
    n_blocks = (n_tokens * TOP_K + N_EXPERTS * rows) // rows
    n_used = (pend[-1:] // rows).astype(jnp.int32)
    blk_start = jnp.arange(n_blocks, dtype=jnp.int32) * rows
    blk_exp = jnp.minimum(jnp.sum(pend[None, :] <= blk_start[:, None], axis=1), N_EXPERTS - 1).astype(jnp.int32)
    expert = rt[:, RT_EXPERT:RT_EXPERT + TOP_K].astype(jnp.int32)
    rank = rt[:, RT_RANK:RT_RANK + TOP_K].astype(jnp.int32)
    first_row = jnp.sum(jnp.where(expert[..., None] == jnp.arange(N_EXPERTS), pstart, 0), axis=-1)
    dest = first_row + rank
    dest_tiles = dest.reshape(n_tokens // tm, tm, TOP_K).transpose(0, 2, 1).reshape(n_tokens // tm, 1, TOP_K * tm)
    return blk_exp, n_used, (pstart + counts).astype(jnp.int32), pend.astype(jnp.int32), dest_tiles, n_blocks * rows


def _layer(x, p_i, i, last_layer, final_norm, attn_norm, w_in, lambda_q1, lambda_k1, lambda_q2, lambda_k2,
           diff_subln, gm_ln_gain, gm_ln_bias, gm_spatial_w, gm_spatial_b, gm_out_norm, w_out, moe_norm,
           w_group_router, w_expert_router, w_expert_gate, w_expert_up, w_expert_down, ple_norm, w_ple_gate,
           b_ple_gate, w_ple_proj):
    batch, seq, d = x.shape
    t = batch * seq
    lambda_init = 0.8 - 0.6 * math.exp(-0.3 * i)
    x2d = x.reshape(t, d)

    perm = _qk_column_perm()
    w_in_b = jnp.concatenate(
        [w_in[:, perm], w_in[:, DA_WIDTH + perm], w_in[:, 2 * DA_WIDTH:]], axis=1).astype(BF16)
    cos_t, sin_t = _rope_tables(seq)
    qt5, k, vt5, o_gm = _in_proj(x2d, attn_norm[None, :], w_in_b, cos_t, sin_t, gm_ln_gain[None, :],
                                 gm_ln_bias[None, :], gm_spatial_w, gm_spatial_b.T, gm_out_norm[None, :], seq)
    lam_params = jnp.stack([lambda_q1, lambda_k1, lambda_q2, lambda_k2]).astype(F32)
    o_da = _attention(lam_params, diff_subln[:, None], qt5, k, vt5, batch, seq, lambda_init)

    w_router = jnp.concatenate([w_group_router, w_expert_router], axis=1)
    w_router_b = jnp.pad(w_router, ((0, 0), (0, LANES - w_router.shape[1]))).astype(BF16)
    x1, h2_tiles, rt, counts_f = _out_proj(x2d, o_da, o_gm, w_out.astype(BF16), moe_norm[None, :], w_router_b)

    blk_exp, n_used, fill_lo, fill_hi, dest_tiles, n_rows = _dispatch_tables(rt, counts_f, t)
    buf = _dispatch(fill_lo, fill_hi, n_used, dest_tiles, h2_tiles, n_rows)
    y_tiles = _moe_ffn(blk_exp, buf, w_expert_gate, w_expert_up, w_expert_down)

    out = _finish(dest_tiles, x1, rt, p_i.reshape(t, -1), ple_norm[None, :], w_ple_gate.astype(BF16),
                  b_ple_gate[None, :], w_ple_proj.astype(BF16), final_norm[None, :], y_tiles, last_layer)
    return out.reshape(batch, seq, d)


def kernel(x, p, attn_norm, w_in, lambda_q1, lambda_k1, lambda_q2, lambda_k2, diff_subln, gm_ln_gain, gm_ln_bias, gm_spatial_w, gm_spatial_b, gm_out_norm, w_out, moe_norm, w_group_router, w_expert_router, w_expert_gate, w_expert_up, w_expert_down, ple_norm, w_ple_gate, b_ple_gate, w_ple_proj, final_norm):
    per_layer = (attn_norm, w_in, lambda_q1, lambda_k1, lambda_q2, lambda_k2, diff_subln, gm_ln_gain, gm_ln_bias,
                 gm_spatial_w, gm_spatial_b, gm_out_norm, w_out, moe_norm, w_group_router, w_expert_router,
                 w_expert_gate, w_expert_up, w_expert_down, ple_norm, w_ple_gate, b_ple_gate, w_ple_proj)
    depth = attn_norm.shape[0]
    for i in range(depth):
        x = _layer(x, p[i], i, i == depth - 1, final_norm, *(w[i] for w in per_layer))
    return x
```

```python
import functools
import math

import numpy as np
import jax
import jax.numpy as jnp
from jax import lax
from jax.experimental import pallas as pl
from jax.experimental.pallas import tpu as pltpu

F32 = jnp.float32
BF16 = jnp.bfloat16

DA_HEADS = 4
DA_HEAD_DIM = 64
DA_VDIM = 2 * DA_HEAD_DIM
DA_WIDTH = DA_HEADS * DA_VDIM
ROPE_THETA = 10000.0
GM_HEADS = 4
GM_HEAD_DIM = 128
GM_WIDTH = GM_HEADS * GM_HEAD_DIM
GM_CHUNK = 128
N_GROUPS = 4
EXPERTS_PER_GROUP = 8
N_EXPERTS = N_GROUPS * EXPERTS_PER_GROUP
TOP_K = 2
EPS = 1e-6

LANES = 128
ROPE_HALF = DA_HEAD_DIM // 2
NEG_BIG = -1e30

ROW_TILE = 512
ATT_TQ = 256
ATT_TK = 512
V_ROWS = DA_VDIM + 16
MOE_ROWS = 512
VMEM_LIMIT = 56 * 1024 * 1024


def _rms(x, g):
    return x * lax.rsqrt(jnp.mean(x * x, axis=-1, keepdims=True) + EPS) * g


def _gelu_tanh(x):
    c = math.sqrt(2.0 / math.pi)
    return x * (0.5 * (1.0 + jnp.tanh(c * (x + 0.044715 * (x * x * x)))))


def _in_proj_body(x_ref, g_ref, w_ref, cos_ref, sin_ref, lng_ref, lnb_ref, ws_ref, bst_ref, og_ref,
                  qt_ref, k_ref, vt_ref, ogm_ref, mix_ref):
    tm = x_ref.shape[0]
    h = _rms(x_ref[...], g_ref[...]).astype(BF16)
    cos = cos_ref[...]
    sin = sin_ref[...]

    zq = jnp.dot(h, w_ref[:, 0:DA_WIDTH], preferred_element_type=F32)
    zk = jnp.dot(h, w_ref[:, DA_WIDTH:2 * DA_WIDTH], preferred_element_type=F32)
    scale = math.log2(math.e) / math.sqrt(DA_HEAD_DIM)
    for hd in range(DA_HEADS):
        sl = slice(hd * LANES, (hd + 1) * LANES)
        qh = zq[:, sl]
        kh = zk[:, sl]
        q_rot = (qh * cos + pltpu.roll(qh, LANES // 2, 1) * sin) * scale
        for c in range(qt_ref.shape[1]):
            qt_ref[hd, c] = q_rot[c * ATT_TQ:(c + 1) * ATT_TQ, :].T.astype(BF16)
        k_ref[:, sl] = (kh * cos + pltpu.roll(kh, LANES // 2, 1) * sin).astype(BF16)

    zv = jnp.dot(h, w_ref[:, 2 * DA_WIDTH:3 * DA_WIDTH], preferred_element_type=F32)
    for hd in range(DA_HEADS):
        vt_ref[hd, 0:DA_VDIM, :] = zv[:, hd * DA_VDIM:(hd + 1) * DA_VDIM].T.astype(BF16)
        vt_ref[hd, DA_VDIM:V_ROWS, :] = jnp.ones((V_ROWS - DA_VDIM, tm), BF16)

    a = _gelu_tanh(jnp.dot(h, w_ref[:, 3 * DA_WIDTH:3 * DA_WIDTH + 2 * GM_WIDTH], preferred_element_type=F32))
    u = a[:, :GM_WIDTH]
    vv = a[:, GM_WIDTH:]
    mu = jnp.mean(vv, axis=-1, keepdims=True)
    vc = vv - mu
    vln = vc * lax.rsqrt(jnp.mean(vc * vc, axis=-1, keepdims=True) + EPS) * lng_ref[...] + lnb_ref[...]
    vb = vln.astype(BF16)
    row = lax.broadcasted_iota(jnp.int32, (GM_CHUNK, GM_CHUNK), 0)
    col = lax.broadcasted_iota(jnp.int32, (GM_CHUNK, GM_CHUNK), 1)
    causal = row >= col
    for hd in range(GM_HEADS):
        wm = jnp.where(causal, ws_ref[hd], 0.0).astype(BF16)
        bias = bst_ref[:, hd:hd + 1]
        cs = slice(hd * GM_HEAD_DIM, (hd + 1) * GM_HEAD_DIM)
        for c in range(tm // GM_CHUNK):
            rs = slice(c * GM_CHUNK, (c + 1) * GM_CHUNK)
            mixed = jnp.dot(wm, vb[rs, cs], preferred_element_type=F32) + bias
            mix_ref[rs, cs] = u[rs, cs] * mixed
    ogm_ref[...] = _rms(mix_ref[...], og_ref[...]).astype(BF16)


def _in_proj(x2d, attn_norm, w_in_b, cos_t, sin_t, ln_g, ln_b, ws, bs_t, out_g, seq):
    t, d = x2d.shape
    tm = ROW_TILE
    assert tm % ATT_TQ == 0 and ATT_TK % tm == 0
    batch = t // seq
    n_pos = seq // tm
    tiles_per_kblk = ATT_TK // tm
    row = lambda i: (i, 0)
    full2 = lambda i: (0, 0)
    return pl.pallas_call(
        _in_proj_body,
        grid=(t // tm,),
        in_specs=[
            pl.BlockSpec((tm, d), row),
            pl.BlockSpec((1, d), full2),
            pl.BlockSpec(w_in_b.shape, full2),
            pl.BlockSpec((tm, LANES), lambda i: (i % n_pos, 0)),
            pl.BlockSpec((tm, LANES), lambda i: (i % n_pos, 0)),
            pl.BlockSpec((1, GM_WIDTH), full2),
            pl.BlockSpec((1, GM_WIDTH), full2),
            pl.BlockSpec(ws.shape, lambda i: (0, 0, 0)),
            pl.BlockSpec(bs_t.shape, full2),
            pl.BlockSpec((1, GM_WIDTH), full2),
        ],
        out_specs=[
            pl.BlockSpec((None, DA_HEADS, tm // ATT_TQ, DA_VDIM, ATT_TQ),
                         lambda i: (i // n_pos, 0, i % n_pos, 0, 0)),
            pl.BlockSpec((tm, DA_WIDTH), row),
            pl.BlockSpec((None, DA_HEADS, None, V_ROWS, tm),
                         lambda i: (i // n_pos, 0, (i % n_pos) // tiles_per_kblk, 0, (i % n_pos) % tiles_per_kblk)),
            pl.BlockSpec((tm, GM_WIDTH), row),
        ],
        out_shape=[
            jax.ShapeDtypeStruct((batch, DA_HEADS, seq // ATT_TQ, DA_VDIM, ATT_TQ), BF16),
            jax.ShapeDtypeStruct((t, DA_WIDTH), BF16),
            jax.ShapeDtypeStruct((batch, DA_HEADS, seq // ATT_TK, V_ROWS, ATT_TK), BF16),
            jax.ShapeDtypeStruct((t, GM_WIDTH), BF16),
        ],
        scratch_shapes=[pltpu.VMEM((tm, GM_WIDTH), F32)],
        compiler_params=pltpu.CompilerParams(dimension_semantics=("parallel",), vmem_limit_bytes=VMEM_LIMIT),
        name="in_proj",
    )(x2d, attn_norm, w_in_b, cos_t, sin_t, ln_g, ln_b, ws, bs_t, out_g)


def _attn_body(lam_ref, g_ref, qt_ref, qtn_ref, k_ref, vt_ref, o_ref, *scratch, tq, tk, lambda_init):
    i = pl.program_id(1)
    nh = qt_ref.shape[0]
    qst_refs, qsn_refs, m_refs, acc_refs, s_refs, bmax_refs = (scratch[n * nh:(n + 1) * nh] for n in range(6))
    cap_ref = scratch[6 * nh]
    caps = cap_ref.shape[0]
    lp = lam_ref[...]
    lam = (jnp.exp(jnp.sum(lp[0:1] * lp[1:2], axis=-1, keepdims=True))
           - jnp.exp(jnp.sum(lp[2:3] * lp[3:4], axis=-1, keepdims=True)) + lambda_init)

    feat = lax.broadcasted_iota(jnp.int32, (DA_VDIM, tq), 0)
    is_map1 = (feat % DA_HEAD_DIM) < ROPE_HALF
    for h in range(nh):
        for src, dst in ((qt_ref, qst_refs), (qtn_ref, qsn_refs)):
            qt = src[h]
            zero = jnp.zeros_like(qt)
            dst[h][:, 0:tq] = jnp.where(is_map1, qt, zero)
            dst[h][:, tq:2 * tq] = jnp.where(is_map1, zero, qt)
        m_refs[h][...] = jnp.full(m_refs[h].shape, NEG_BIG, F32)
        acc_refs[h][...] = jnp.zeros(acc_refs[h].shape, F32)

    def produce_scores(h, q_refs, j):
        row0 = pl.multiple_of(j * tk, tk)
        kj = k_ref[pl.ds(row0, tk), h * LANES:(h + 1) * LANES]
        sc = jnp.dot(kj, q_refs[h][...], preferred_element_type=F32)
        s_refs[h][...] = sc
        bmax_refs[h][...] = jnp.max(sc, axis=0, keepdims=True)

    def step(j, masked):
        for h in range(nh):
            s = s_refs[h][...]
            if masked:
                s = jnp.minimum(s, cap_ref[i % caps])
                bmax = jnp.max(s, axis=0, keepdims=True)
            else:
                bmax = bmax_refs[h][...]
            m_old = m_refs[h][...]
            m_new = jnp.maximum(m_old, bmax)
            alpha = jnp.exp2(m_old - m_new)
            p = jnp.exp2(s - m_new)
            m_refs[h][...] = m_new
            if masked:
                produce_scores(h, qsn_refs, 0)
            else:
                produce_scores(h, qst_refs, j + 1)
            acc_refs[h][...] = (acc_refs[h][...] * alpha
                                + jnp.dot(vt_ref[h, j], p.astype(BF16), preferred_element_type=F32))

    n_full = (i * tq) // tk

    @pl.when(i == 0)
    def _():
        for h in range(nh):
            produce_scores(h, qst_refs, 0)
        r_i = lax.broadcasted_iota(jnp.int32, (tk, 2 * tq), 0)
        c_i = lax.broadcasted_iota(jnp.int32, (tk, 2 * tq), 1) % tq
        for par in range(caps):
            cap_ref[par] = jnp.where(r_i - c_i <= par * tq, -NEG_BIG, NEG_BIG)

    def two_steps(jj, c):
        step(2 * jj, False)
        step(2 * jj + 1, False)
        return c
    lax.fori_loop(0, n_full // 2, two_steps, 0)

    @pl.when(n_full % 2 == 1)
    def _():
        step(n_full - 1, False)
    step(n_full, True)

    for h in range(nh):
        o = acc_refs[h][0:DA_VDIM, :] * (1.0 / acc_refs[h][DA_VDIM:DA_VDIM + 1, :])
        od = o[:, :tq] - lam * o[:, tq:]
        y = od * lax.rsqrt(jnp.mean(od * od, axis=0, keepdims=True) + EPS) * g_ref[...] * (1.0 - lambda_init)
        o_ref[:, h * LANES:(h + 1) * LANES] = y.T.astype(BF16)


def _attention(lam_params, subln_col, qt5, k2d, vt5, batch, seq, lambda_init):
    tq, tk = ATT_TQ, ATT_TK
    nq, nk = seq // tq, seq // tk
    nh = DA_HEADS
    body = functools.partial(_attn_body, tq=tq, tk=tk, lambda_init=lambda_init)

    def next_q(b, i):
        f = jnp.minimum(b * nq + i + 1, batch * nq - 1)
        return (f // nq, 0, f % nq, 0, 0)
    return pl.pallas_call(
        body,
        grid=(batch, nq),
        in_specs=[
            pl.BlockSpec(lam_params.shape, lambda b, i: (0, 0)),
            pl.BlockSpec(subln_col.shape, lambda b, i: (0, 0)),
            pl.BlockSpec((None, nh, None, DA_VDIM, tq), lambda b, i: (b, 0, i, 0, 0)),
            pl.BlockSpec((None, nh, None, DA_VDIM, tq), next_q),
            pl.BlockSpec((seq, DA_WIDTH), lambda b, i: (b, 0)),
            pl.BlockSpec((None, nh, nk, V_ROWS, tk), lambda b, i: (b, 0, 0, 0, 0)),
        ],
        out_specs=pl.BlockSpec((tq, DA_WIDTH), lambda b, i: (b * nq + i, 0)),
        out_shape=jax.ShapeDtypeStruct((batch * seq, DA_WIDTH), BF16),
        scratch_shapes=(
            [pltpu.VMEM((DA_VDIM, 2 * tq), BF16)] * (2 * nh)
            + [pltpu.VMEM((1, 2 * tq), F32)] * nh
            + [pltpu.VMEM((V_ROWS, 2 * tq), F32)] * nh
            + [pltpu.VMEM((tk, 2 * tq), F32)] * nh
            + [pltpu.VMEM((1, 2 * tq), F32)] * nh
            + [pltpu.VMEM((tk // tq, tk, 2 * tq), F32)]),
        compiler_params=pltpu.CompilerParams(
            dimension_semantics=("arbitrary", "arbitrary"), vmem_limit_bytes=VMEM_LIMIT),
        name="diff_attention",
    )(lam_params, subln_col, qt5, qt5, k2d, vt5)


RT_GATE = 0
RT_EXPERT = 2
RT_RANK = 4
ROUTER_EXPERT_LANE0 = N_GROUPS


TILE_ROWS = 8


def _to_row_tiles(ref, value):
    rows = value.shape[0]
    for c in range(TILE_ROWS):
        ref[pl.ds(c, rows, stride=TILE_ROWS), :] = value[:, c * LANES:(c + 1) * LANES]


def _from_row_tiles(ref, row0, rows):
    return jnp.concatenate(
        [ref[pl.ds(row0 * TILE_ROWS + c, rows, stride=TILE_ROWS), :] for c in range(TILE_ROWS)], axis=1)


def _tile_of(ref, row):
    return ref.at[pl.ds(pl.multiple_of(row * TILE_ROWS, TILE_ROWS), TILE_ROWS)]


def _out_proj_body(x_ref, oda_ref, ogm_ref, w_ref, g_ref, wr_ref, x1_ref, h2_ref, rt_ref, cnt_ref, carry_ref):
    tm = x_ref.shape[0]

    @pl.when(pl.program_id(0) == 0)
    def _():
        carry_ref[...] = jnp.zeros(carry_ref.shape, F32)

    x1 = (x_ref[...]
          + jnp.dot(oda_ref[...], w_ref[0:DA_WIDTH, :], preferred_element_type=F32)
          + jnp.dot(ogm_ref[...], w_ref[DA_WIDTH:DA_WIDTH + GM_WIDTH, :], preferred_element_type=F32))
    x1_ref[...] = x1
    h2 = _rms(x1, g_ref[...])
    _to_row_tiles(h2_ref, h2)
    lg = jnp.dot(h2.astype(BF16), wr_ref[...], preferred_element_type=F32)

    lane = lax.broadcasted_iota(jnp.int32, lg.shape, 1).astype(F32)
    ninf = -jnp.inf
    first_lane = lambda hit: jnp.min(jnp.where(hit, lane, float(LANES)), axis=-1, keepdims=True)
    is_grp = lane < N_GROUPS
    gl = jnp.where(is_grp, lg, ninf)
    gmax = jnp.max(gl, axis=-1, keepdims=True)
    g_idx = first_lane(gl == gmax)
    g_gate = 1.0 / jnp.sum(jnp.where(is_grp, jnp.exp(lg - gmax), 0.0), axis=-1, keepdims=True)
    e_of_lane = lane - ROUTER_EXPERT_LANE0
    e_lo = g_idx * EXPERTS_PER_GROUP
    in_grp = (e_of_lane >= e_lo) & (e_of_lane < e_lo + EXPERTS_PER_GROUP)
    el = jnp.where(in_grp, lg, ninf)
    v1 = jnp.max(el, axis=-1, keepdims=True)
    i1 = first_lane(el == v1)
    el2 = jnp.where(lane == i1, ninf, el)
    v2 = jnp.max(el2, axis=-1, keepdims=True)
    i2 = first_lane(el2 == v2)
    t21 = jnp.exp(v2 - v1)
    w1 = g_gate / (1.0 + t21)
    w2 = g_gate * t21 / (1.0 + t21)

    hit1 = lane == i1
    hit2 = lane == i2
    chosen = (hit1 | hit2).astype(BF16)
    r_i = lax.broadcasted_iota(jnp.int32, (tm, tm), 0)
    c_i = lax.broadcasted_iota(jnp.int32, (tm, tm), 1)
    before = jnp.where(r_i > c_i, 1.0, 0.0).astype(BF16)
    base = carry_ref[...] + jnp.dot(before, chosen, preferred_element_type=F32)
    rank1 = jnp.sum(jnp.where(hit1, base, 0.0), axis=-1, keepdims=True)
    rank2 = jnp.sum(jnp.where(hit2, base, 0.0), axis=-1, keepdims=True)
    carry_ref[...] = carry_ref[...] + jnp.sum(chosen.astype(F32), axis=0, keepdims=True)
    cnt_ref[...] = carry_ref[...]

    rec = jnp.zeros(lg.shape, F32)
    for k, val in ((RT_GATE, w1), (RT_GATE + 1, w2),
                   (RT_EXPERT, (i1 - ROUTER_EXPERT_LANE0).astype(F32)),
                   (RT_EXPERT + 1, (i2 - ROUTER_EXPERT_LANE0).astype(F32)),
                   (RT_RANK, rank1), (RT_RANK + 1, rank2)):
        rec = jnp.where(lane == k, val, rec)
    rt_ref[...] = rec


def _out_proj(x2d, o_da, o_gm, w_out_b, moe_norm, w_router_b):
    t, d = x2d.shape
    tm = ROW_TILE
    row = lambda i: (i, 0)
    full2 = lambda i: (0, 0)
    return pl.pallas_call(
        _out_proj_body,
        grid=(t // tm,),
        in_specs=[
            pl.BlockSpec((tm, d), row),
            pl.BlockSpec((tm, DA_WIDTH), row),
            pl.BlockSpec((tm, GM_WIDTH), row),
            pl.BlockSpec(w_out_b.shape, full2),
            pl.BlockSpec((1, d), full2),
            pl.BlockSpec(w_router_b.shape, full2),
        ],
        out_specs=[
            pl.BlockSpec((tm, d), row),
            pl.BlockSpec((tm * TILE_ROWS, LANES), row),
            pl.BlockSpec((tm, LANES), row),
            pl.BlockSpec((1, LANES), full2),
        ],
        out_shape=[
            jax.ShapeDtypeStruct((t, d), F32),
            jax.ShapeDtypeStruct((t * TILE_ROWS, LANES), F32),
            jax.ShapeDtypeStruct((t, LANES), F32),
            jax.ShapeDtypeStruct((1, LANES), F32),
        ],
        scratch_shapes=[pltpu.VMEM((1, LANES), F32)],
        compiler_params=pltpu.CompilerParams(dimension_semantics=("arbitrary",), vmem_limit_bytes=VMEM_LIMIT),
        name="out_proj",
    )(x2d, o_da, o_gm, w_out_b, moe_norm, w_router_b)


DMA_GROUP = 8


ZERO_RUN = 32


def _dispatch_body(fill_lo_ref, fill_hi_ref, n_used_ref, dst_ref, h2_ref, buf_hbm, stage, zblk, sem, zsem, rsem,
                   bsem, *, tm, n_blocks):
    i = pl.program_id(0)
    n = pl.num_programs(0)
    slot = i % 2
    stage[slot] = h2_ref[...]

    def row_copy(r, dst, s):
        return pltpu.make_async_copy(_tile_of(stage.at[s], r), _tile_of(buf_hbm, dst), sem.at[s])

    def issue(g, c):
        r0 = g * DMA_GROUP
        dst = [[dst_ref[0, 0, k * tm + r0 + u] for k in range(TOP_K)] for u in range(DMA_GROUP)]
        for u in range(DMA_GROUP):
            for k in range(TOP_K):
                row_copy(r0 + u, dst[u][k], slot).start(priority=k % 2)
        return c
    lax.fori_loop(0, tm // DMA_GROUP, issue, 0)

    def wait_tile(s):
        whole = pltpu.make_async_copy(stage.at[s], buf_hbm.at[pl.ds(0, stage.shape[1])], sem.at[s])
        for _ in range(TOP_K):
            whole.wait()

    @pl.when(i > 0)
    def _():
        wait_tile(1 - slot)

    @pl.when(i == n - 1)
    def _():
        wait_tile(slot)
        zblk[...] = jnp.zeros(zblk.shape, F32)
        fill = lambda r: pltpu.make_async_copy(zblk.at[pl.ds(0, TILE_ROWS)], _tile_of(buf_hbm, r), zsem)
        run_tiles = ZERO_RUN * TILE_ROWS
        fill_run = lambda r0: pltpu.make_async_copy(
            zblk.at[pl.ds(0, run_tiles)],
            buf_hbm.at[pl.ds(pl.multiple_of(r0 * TILE_ROWS, TILE_ROWS), run_tiles)], rsem)
        blk_tiles = zblk.shape[0]
        fill_blk = lambda b: pltpu.make_async_copy(
            zblk, buf_hbm.at[pl.ds(pl.multiple_of(b * blk_tiles, blk_tiles), blk_tiles)], bsem)

        def do(copy_of, method):
            def body(r, c):
                getattr(copy_of(r), method)()
                return c
            return body
        for method in ("start", "wait"):
            for e in range(N_EXPERTS):
                lo, hi = fill_lo_ref[e], fill_hi_ref[e]
                runs = lax.shift_right_logical(hi - lo, ZERO_RUN.bit_length() - 1)
                lax.fori_loop(0, runs, do(lambda c, hi=hi: fill_run(hi - (c + 1) * ZERO_RUN), method), 0)
                lax.fori_loop(lo, hi - runs * ZERO_RUN, do(fill, method), 0)
            lax.fori_loop(n_used_ref[0], n_blocks, do(fill_blk, method), 0)


def _dispatch(fill_lo, fill_hi, n_used, dest_tiles, h2_tiles, n_rows):
    t = h2_tiles.shape[0] // TILE_ROWS
    tm = ROW_TILE
    grid_spec = pltpu.PrefetchScalarGridSpec(
        num_scalar_prefetch=3,
        grid=(t // tm,),
        in_specs=[
            pl.BlockSpec((1, 1, TOP_K * tm), lambda i, lo, hi, nu: (i, 0, 0), memory_space=pltpu.SMEM),
            pl.BlockSpec((tm * TILE_ROWS, LANES), lambda i, lo, hi, nu: (i, 0)),
        ],
        out_specs=pl.BlockSpec(memory_space=pl.ANY),
        scratch_shapes=[
            pltpu.VMEM((2, tm * TILE_ROWS, LANES), F32),
            pltpu.VMEM((MOE_ROWS * TILE_ROWS, LANES), F32),
            pltpu.SemaphoreType.DMA((2,)),
            pltpu.SemaphoreType.DMA,
            pltpu.SemaphoreType.DMA,
            pltpu.SemaphoreType.DMA,
        ],
    )
    return pl.pallas_call(
        functools.partial(_dispatch_body, tm=tm, n_blocks=n_rows // MOE_ROWS),
        grid_spec=grid_spec,
        out_shape=jax.ShapeDtypeStruct((n_rows * TILE_ROWS, LANES), F32),
        compiler_params=pltpu.CompilerParams(dimension_semantics=("arbitrary",), vmem_limit_bytes=VMEM_LIMIT),
        name="dispatch",
    )(fill_lo, fill_hi, n_used, dest_tiles, h2_tiles)


def _moe_body(blk_exp_ref, x_ref, wg_ref, wu_ref, wd_ref, y_ref, wg_b, wu_b, wd_b):
    i = pl.program_id(0)

    @pl.when((i == 0) | (blk_exp_ref[i] != blk_exp_ref[jnp.maximum(i - 1, 0)]))
    def _():
        wg_b[...] = wg_ref[0].astype(BF16)
        wu_b[...] = wu_ref[0].astype(BF16)
        wd_b[...] = wd_ref[0].astype(BF16)

    xb = _from_row_tiles(x_ref, 0, MOE_ROWS).astype(BF16)
    g = jnp.dot(xb, wg_b[...], preferred_element_type=F32)
    u = jnp.dot(xb, wu_b[...], preferred_element_type=F32)
    act = (g * jax.nn.sigmoid(g) * u).astype(BF16)
    _to_row_tiles(y_ref, jnp.dot(act, wd_b[...], preferred_element_type=F32))


def _moe_ffn(blk_exp, buf, wg, wu, wd):
    n_rows = buf.shape[0] // TILE_ROWS
    d, f = wg.shape[1:]
    rows = MOE_ROWS
    row_blk = lambda i, be: (i, 0)
    w_blk = lambda i, be: (be[i], 0, 0)
    grid_spec = pltpu.PrefetchScalarGridSpec(
        num_scalar_prefetch=1,
        grid=(n_rows // rows,),
        in_specs=[
            pl.BlockSpec((rows * TILE_ROWS, LANES), row_blk),
            pl.BlockSpec((1, d, f), w_blk),
            pl.BlockSpec((1, d, f), w_blk),
            pl.BlockSpec((1, f, d), w_blk),
        ],
        out_specs=pl.BlockSpec((rows * TILE_ROWS, LANES), row_blk),
        scratch_shapes=[pltpu.VMEM((d, f), BF16), pltpu.VMEM((d, f), BF16), pltpu.VMEM((f, d), BF16)],
    )
    return pl.pallas_call(
        _moe_body,
        grid_spec=grid_spec,
        out_shape=jax.ShapeDtypeStruct(buf.shape, F32),
        compiler_params=pltpu.CompilerParams(dimension_semantics=("arbitrary",), vmem_limit_bytes=VMEM_LIMIT),
        name="moe_ffn",
    )(blk_exp, buf, wg, wu, wd)


GATHER_AHEAD = 2


def _finish_body(*refs, last_layer):
    src_refs = refs[:GATHER_AHEAD + 1]
    x1_ref, rt_ref, p_ref, pn_ref, wg_ref, bg_ref, wp_ref, fn_ref, y_hbm, o_ref, ybuf, sem = refs[GATHER_AHEAD + 1:]
    tm = x1_ref.shape[0]
    i = pl.program_id(0)
    n = pl.num_programs(0)
    slots = GATHER_AHEAD + 1
    slot = lax.rem(i, slots)

    def row_copy(src, r, s):
        return pltpu.make_async_copy(_tile_of(y_hbm, src), _tile_of(ybuf.at[s], r), sem.at[s])

    def gather_group(idx_ref, s, r0):
        src = [idx_ref[0, 0, r0 + u] for u in range(TOP_K * DMA_GROUP)]
        for u in range(TOP_K * DMA_GROUP):
            row_copy(src[u], r0 + u, s).start(priority=u % 2)

    @pl.when(i == 0)
    def _():
        for a in range(GATHER_AHEAD):
            def body(g, c, a=a):
                gather_group(src_refs[a], a, g * (TOP_K * DMA_GROUP))
                return c
            lax.fori_loop(0, tm // DMA_GROUP, body, 0)

    def wait_slot(s):
        pltpu.make_async_copy(y_hbm.at[pl.ds(0, ybuf.shape[1])], ybuf.at[s], sem.at[s]).wait()

    wait_slot(slot)

    ahead_slot = lax.rem(i + GATHER_AHEAD, slots)
    for g in range(tm // DMA_GROUP):
        gather_group(src_refs[GATHER_AHEAD], ahead_slot, g * (TOP_K * DMA_GROUP))

    rt = rt_ref[...]
    y0 = _from_row_tiles(ybuf.at[slot], 0, tm)
    y1 = _from_row_tiles(ybuf.at[slot], tm, tm)
    x2 = x1_ref[...] + rt[:, RT_GATE:RT_GATE + 1] * y0 + rt[:, RT_GATE + 1:RT_GATE + 2] * y1
    hg = _rms(x2, pn_ref[...]).astype(BF16)
    gate = jax.nn.sigmoid(jnp.dot(hg, wg_ref[...], preferred_element_type=F32) + bg_ref[...])
    x3 = x2 + gate * jnp.dot(p_ref[...].astype(BF16), wp_ref[...], preferred_element_type=F32)
    o_ref[...] = _rms(x3, fn_ref[...]) if last_layer else x3

    @pl.when(i == n - 1)
    def _():
        for a in range(1, GATHER_AHEAD + 1):
            wait_slot(lax.rem(i + a, slots))


def _finish(dest_tiles, x1, rt, p2d, ple_norm, w_gate_b, b_gate, w_proj_b, final_norm, y_tiles, last_layer):
    t, d = x1.shape
    tm = ROW_TILE
    n_tiles = t // tm
    row = lambda i: (i, 0)
    full2 = lambda i: (0, 0)
    return pl.pallas_call(
        functools.partial(_finish_body, last_layer=last_layer),
        grid=(n_tiles,),
        in_specs=[
            pl.BlockSpec((1, 1, TOP_K * tm), lambda i, a=a: (jnp.minimum(i + a, n_tiles - 1), 0, 0),
                         memory_space=pltpu.SMEM)
            for a in range(GATHER_AHEAD + 1)
        ] + [
            pl.BlockSpec((tm, d), row),
            pl.BlockSpec((tm, LANES), row),
            pl.BlockSpec((tm, p2d.shape[1]), row),
            pl.BlockSpec((1, d), full2),
            pl.BlockSpec(w_gate_b.shape, full2),
            pl.BlockSpec((1, d), full2),
            pl.BlockSpec(w_proj_b.shape, full2),
            pl.BlockSpec((1, d), full2),
            pl.BlockSpec(memory_space=pl.ANY),
        ],
        out_specs=pl.BlockSpec((tm, d), row),
        out_shape=jax.ShapeDtypeStruct((t, d), F32),
        scratch_shapes=[
            pltpu.VMEM((GATHER_AHEAD + 1, TOP_K * tm * TILE_ROWS, LANES), F32),
            pltpu.SemaphoreType.DMA((GATHER_AHEAD + 1,)),
        ],
        compiler_params=pltpu.CompilerParams(dimension_semantics=("arbitrary",), vmem_limit_bytes=VMEM_LIMIT),
        name="finish",
    )(*([dest_tiles] * (GATHER_AHEAD + 1)), x1, rt, p2d, ple_norm, w_gate_b, b_gate, w_proj_b, final_norm, y_tiles)


def _qk_column_perm():
    perm = np.zeros((DA_WIDTH,), np.int32)
    for h in range(DA_HEADS):
        for half in range(2):
            for mp in range(2):
                for i in range(ROPE_HALF):
                    perm[h * LANES + half * 64 + mp * ROPE_HALF + i] = h * LANES + mp * DA_HEAD_DIM + half * ROPE_HALF + i
    return perm


def _rope_tables(seq):
    inv = 1.0 / (ROPE_THETA ** (jnp.arange(0, DA_HEAD_DIM, 2, dtype=F32) / DA_HEAD_DIM))
    ang = jnp.arange(seq, dtype=F32)[:, None] * inv[None, :]
    cos = jnp.tile(jnp.cos(ang), (1, LANES // ROPE_HALF))
    sin = jnp.tile(jnp.sin(ang), (1, LANES // ROPE_HALF))
    sign = jnp.where(jnp.arange(LANES) < LANES // 2, -1.0, 1.0).astype(F32)
    return cos, sin * sign[None, :]


def _dispatch_tables(rt, counts_f, n_tokens):
    rows = MOE_ROWS
    tm = ROW_TILE
    counts = counts_f[0, ROUTER_EXPERT_LANE0:ROUTER_EXPERT_LANE0 + N_EXPERTS].astype(jnp.int32)
    padded = ((counts + rows - 1) // rows) * rows
    pend = jnp.cumsum(padded)
    pstart = pend - padded
    n_blocks = (n_tokens * TOP_K + N_EXPERTS * rows) // rows
    n_used = (pend[-1:] // rows).astype(jnp.int32)
    blk_start = jnp.arange(n_blocks, dtype=jnp.int32) * rows
    blk_exp = jnp.minimum(jnp.sum(pend[None, :] <= blk_start[:, None], axis=1), N_EXPERTS - 1).astype(jnp.int32)
    expert = rt[:, RT_EXPERT:RT_EXPERT + TOP_K].astype(jnp.int32)
    rank = rt[:, RT_RANK:RT_RANK + TOP_K].astype(jnp.int32)
    first_row = jnp.sum(jnp.where(expert[..., None] == jnp.arange(N_EXPERTS), pstart, 0), axis=-1)
    dest = first_row + rank
    dest_tiles = dest.reshape(n_tokens // tm, tm, TOP_K).transpose(0, 2, 1).reshape(n_tokens // tm, 1, TOP_K * tm)
    return blk_exp, n_used, (pstart + counts).astype(jnp.int32), pend.astype(jnp.int32), dest_tiles, n_blocks * rows


def _layer(x, p_i, i, last_layer, final_norm, attn_norm, w_in, lambda_q1, lambda_k1, lambda_q2, lambda_k2,
           diff_subln, gm_ln_gain, gm_ln_bias, gm_spatial_w, gm_spatial_b, gm_out_norm, w_out, moe_norm,
           w_group_router, w_expert_router, w_expert_gate, w_expert_up, w_expert_down, ple_norm, w_ple_gate,
           b_ple_gate, w_ple_proj):
    batch, seq, d = x.shape
    t = batch * seq
    lambda_init = 0.8 - 0.6 * math.exp(-0.3 * i)
    x2d = x.reshape(t, d)

    perm = _qk_column_perm()
    w_in_b = jnp.concatenate(
        [w_in[:, perm], w_in[:, DA_WIDTH + perm], w_in[:, 2 * DA_WIDTH:]], axis=1).astype(BF16)
    cos_t, sin_t = _rope_tables(seq)
    qt5, k, vt5, o_gm = _in_proj(x2d, attn_norm[None, :], w_in_b, cos_t, sin_t, gm_ln_gain[None, :],
                                 gm_ln_bias[None, :], gm_spatial_w, gm_spatial_b.T, gm_out_norm[None, :], seq)
    lam_params = jnp.stack([lambda_q1, lambda_k1, lambda_q2, lambda_k2]).astype(F32)
    o_da = _attention(lam_params, diff_subln[:, None], qt5, k, vt5, batch, seq, lambda_init)

    w_router = jnp.concatenate([w_group_router, w_expert_router], axis=1)
    w_router_b = jnp.pad(w_router, ((0, 0), (0, LANES - w_router.shape[1]))).astype(BF16)
    x1, h2_tiles, rt, counts_f = _out_proj(x2d, o_da, o_gm, w_out.astype(BF16), moe_norm[None, :], w_router_b)

    blk_exp, n_used, fill_lo, fill_hi, dest_tiles, n_rows = _dispatch_tables(rt, counts_f, t)
    buf = _dispatch(fill_lo, fill_hi, n_used, dest_tiles, h2_tiles, n_rows)
    y_tiles = _moe_ffn(blk_exp, buf, w_expert_gate, w_expert_up, w_expert_down)

    out = _finish(dest_tiles, x1, rt, p_i.reshape(t, -1), ple_norm[None, :], w_ple_gate.astype(BF16),
                  b_ple_gate[None, :], w_ple_proj.astype(BF16), final_norm[None, :], y_tiles, last_layer)
    return out.reshape(batch, seq, d)


def kernel(x, p, attn_norm, w_in, lambda_q1, lambda_k1, lambda_q2, lambda_k2, diff_subln, gm_ln_gain, gm_ln_bias, gm_spatial_w, gm_spatial_b, gm_out_norm, w_out, moe_norm, w_group_router, w_expert_router, w_expert_gate, w_expert_up, w_expert_down, ple_norm, w_ple_gate, b_ple_gate, w_ple_proj, final_norm):
    per_layer = (attn_norm, w_in, lambda_q1, lambda_k1, lambda_q2, lambda_k2, diff_subln, gm_ln_gain, gm_ln_bias,
                 gm_spatial_w, gm_spatial_b, gm_out_norm, w_out, moe_norm, w_group_router, w_expert_router,
                 w_expert_gate, w_expert_up, w_expert_down, ple_norm, w_ple_gate, b_ple_gate, w_ple_proj)
    depth = attn_norm.shape[0]
    for i in range(depth):
        x = _layer(x, p[i], i, i == depth - 1, final_norm, *(w[i] for w in per_layer))
    return x
```

```python
import functools
import math

import numpy as np
import jax
import jax.numpy as jnp
from jax import lax
from jax.experimental import pallas as pl
from jax.experimental.pallas import tpu as pltpu

F32 = jnp.float32
BF16 = jnp.bfloat16

DA_HEADS = 4
DA_HEAD_DIM = 64
DA_VDIM = 2 * DA_HEAD_DIM
DA_WIDTH = DA_HEADS * DA_VDIM
ROPE_THETA = 10000.0
GM_HEADS = 4
GM_HEAD_DIM = 128
GM_WIDTH = GM_HEADS * GM_HEAD_DIM
GM_CHUNK = 128
N_GROUPS = 4
EXPERTS_PER_GROUP = 8
N_EXPERTS = N_GROUPS * EXPERTS_PER_GROUP
TOP_K = 2
EPS = 1e-6

LANES = 128
ROPE_HALF = DA_HEAD_DIM // 2
NEG_BIG = -1e30

ROW_TILE = 512
ATT_TQ = 256
ATT_TK = 512
V_ROWS = DA_VDIM + 16
MOE_ROWS = 512
VMEM_LIMIT = 56 * 1024 * 1024


def _rms(x, g):
    return x * lax.rsqrt(jnp.mean(x * x, axis=-1, keepdims=True) + EPS) * g


def _gelu_tanh(x):
    c = math.sqrt(2.0 / math.pi)
    return x * (0.5 * (1.0 + jnp.tanh(c * (x + 0.044715 * (x * x * x)))))


def _in_proj_body(x_ref, g_ref, w_ref, cos_ref, sin_ref, lng_ref, lnb_ref, ws_ref, bst_ref, og_ref,
                  qt_ref, k_ref, vt_ref, ogm_ref, mix_ref):
    tm = x_ref.shape[0]
    h = _rms(x_ref[...], g_ref[...]).astype(BF16)
    cos = cos_ref[...]
    sin = sin_ref[...]

    zq = jnp.dot(h, w_ref[:, 0:DA_WIDTH], preferred_element_type=F32)
    zk = jnp.dot(h, w_ref[:, DA_WIDTH:2 * DA_WIDTH], preferred_element_type=F32)
    scale = math.log2(math.e) / math.sqrt(DA_HEAD_DIM)
    for hd in range(DA_HEADS):
        sl = slice(hd * LANES, (hd + 1) * LANES)
        qh = zq[:, sl]
        kh = zk[:, sl]
        q_rot = (qh * cos + pltpu.roll(qh, LANES // 2, 1) * sin) * scale
        for c in range(qt_ref.shape[1]):
            qt_ref[hd, c] = q_rot[c * ATT_TQ:(c + 1) * ATT_TQ, :].T.astype(BF16)
        k_ref[:, sl] = (kh * cos + pltpu.roll(kh, LANES // 2, 1) * sin).astype(BF16)

    zv = jnp.dot(h, w_ref[:, 2 * DA_WIDTH:3 * DA_WIDTH], preferred_element_type=F32)
    for hd in range(DA_HEADS):
        vt_ref[hd, 0:DA_VDIM, :] = zv[:, hd * DA_VDIM:(hd + 1) * DA_VDIM].T.astype(BF16)
        vt_ref[hd, DA_VDIM:V_ROWS, :] = jnp.ones((V_ROWS - DA_VDIM, tm), BF16)

    a = _gelu_tanh(jnp.dot(h, w_ref[:, 3 * DA_WIDTH:3 * DA_WIDTH + 2 * GM_WIDTH], preferred_element_type=F32))
    u = a[:, :GM_WIDTH]
    vv = a[:, GM_WIDTH:]
    mu = jnp.mean(vv, axis=-1, keepdims=True)
    vc = vv - mu
    vln = vc * lax.rsqrt(jnp.mean(vc * vc, axis=-1, keepdims=True) + EPS) * lng_ref[...] + lnb_ref[...]
    vb = vln.astype(BF16)
    row = lax.broadcasted_iota(jnp.int32, (GM_CHUNK, GM_CHUNK), 0)
    col = lax.broadcasted_iota(jnp.int32, (GM_CHUNK, GM_CHUNK), 1)
    causal = row >= col
    for hd in range(GM_HEADS):
        wm = jnp.where(causal, ws_ref[hd], 0.0).astype(BF16)
        bias = bst_ref[:, hd:hd + 1]
        cs = slice(hd * GM_HEAD_DIM, (hd + 1) * GM_HEAD_DIM)
        for c in range(tm // GM_CHUNK):
            rs = slice(c * GM_CHUNK, (c + 1) * GM_CHUNK)
            mixed = jnp.dot(wm, vb[rs, cs], preferred_element_type=F32) + bias
            mix_ref[rs, cs] = u[rs, cs] * mixed
    ogm_ref[...] = _rms(mix_ref[...], og_ref[...]).astype(BF16)


def _in_proj(x2d, attn_norm, w_in_b, cos_t, sin_t, ln_g, ln_b, ws, bs_t, out_g, seq):
    t, d = x2d.shape
    tm = ROW_TILE
    assert tm % ATT_TQ == 0 and ATT_TK % tm == 0
    batch = t // seq
    n_pos = seq // tm
    tiles_per_kblk = ATT_TK // tm
    row = lambda i: (i, 0)
    full2 = lambda i: (0, 0)
    return pl.pallas_call(
        _in_proj_body,
        grid=(t // tm,),
        in_specs=[
            pl.BlockSpec((tm, d), row),
            pl.BlockSpec((1, d), full2),
            pl.BlockSpec(w_in_b.shape, full2),
            pl.BlockSpec((tm, LANES), lambda i: (i % n_pos, 0)),
            pl.BlockSpec((tm, LANES), lambda i: (i % n_pos, 0)),
            pl.BlockSpec((1, GM_WIDTH), full2),
            pl.BlockSpec((1, GM_WIDTH), full2),
            pl.BlockSpec(ws.shape, lambda i: (0, 0, 0)),
            pl.BlockSpec(bs_t.shape, full2),
            pl.BlockSpec((1, GM_WIDTH), full2),
        ],
        out_specs=[
            pl.BlockSpec((None, DA_HEADS, tm // ATT_TQ, DA_VDIM, ATT_TQ),
                         lambda i: (i // n_pos, 0, i % n_pos, 0, 0)),
            pl.BlockSpec((tm, DA_WIDTH), row),
            pl.BlockSpec((None, DA_HEADS, None, V_ROWS, tm),
                         lambda i: (i // n_pos, 0, (i % n_pos) // tiles_per_kblk, 0, (i % n_pos) % tiles_per_kblk)),
            pl.BlockSpec((tm, GM_WIDTH), row),
        ],
        out_shape=[
            jax.ShapeDtypeStruct((batch, DA_HEADS, seq // ATT_TQ, DA_VDIM, ATT_TQ), BF16),
            jax.ShapeDtypeStruct((t, DA_WIDTH), BF16),
            jax.ShapeDtypeStruct((batch, DA_HEADS, seq // ATT_TK, V_ROWS, ATT_TK), BF16),
            jax.ShapeDtypeStruct((t, GM_WIDTH), BF16),
        ],
        scratch_shapes=[pltpu.VMEM((tm, GM_WIDTH), F32)],
        compiler_params=pltpu.CompilerParams(dimension_semantics=("parallel",), vmem_limit_bytes=VMEM_LIMIT),
        name="in_proj",
    )(x2d, attn_norm, w_in_b, cos_t, sin_t, ln_g, ln_b, ws, bs_t, out_g)


def _attn_body(lam_ref, g_ref, qt_ref, qtn_ref, k_ref, vt_ref, o_ref, *scratch, tq, tk, lambda_init):
    i = pl.program_id(1)
    nh = qt_ref.shape[0]
    qst_refs, qsn_refs, m_refs, acc_refs, s_refs, bmax_refs = (scratch[n * nh:(n + 1) * nh] for n in range(6))
    cap_ref = scratch[6 * nh]
    caps = cap_ref.shape[0]
    lp = lam_ref[...]
    lam = (jnp.exp(jnp.sum(lp[0:1] * lp[1:2], axis=-1, keepdims=True))
           - jnp.exp(jnp.sum(lp[2:3] * lp[3:4], axis=-1, keepdims=True)) + lambda_init)

    feat = lax.broadcasted_iota(jnp.int32, (DA_VDIM, tq), 0)
    is_map1 = (feat % DA_HEAD_DIM) < ROPE_HALF
    for h in range(nh):
        for src, dst in ((qt_ref, qst_refs), (qtn_ref, qsn_refs)):
            qt = src[h]
            zero = jnp.zeros_like(qt)
            dst[h][:, 0:tq] = jnp.where(is_map1, qt, zero)
            dst[h][:, tq:2 * tq] = jnp.where(is_map1, zero, qt)
        m_refs[h][...] = jnp.full(m_refs[h].shape, NEG_BIG, F32)
        acc_refs[h][...] = jnp.zeros(acc_refs[h].shape, F32)

    def produce_scores(h, q_refs, j):
        row0 = pl.multiple_of(j * tk, tk)
        kj = k_ref[pl.ds(row0, tk), h * LANES:(h + 1) * LANES]
        sc = jnp.dot(kj, q_refs[h][...], preferred_element_type=F32)
        s_refs[h][...] = sc
        bmax_refs[h][...] = jnp.max(sc, axis=0, keepdims=True)

    def step(j, masked):
        for h in range(nh):
            s = s_refs[h][...]
            if masked:
                s = jnp.minimum(s, cap_ref[i % caps])
                bmax = jnp.max(s, axis=0, keepdims=True)
            else:
                bmax = bmax_refs[h][...]
            m_old = m_refs[h][...]
            m_new = jnp.maximum(m_old, bmax)
            alpha = jnp.exp2(m_old - m_new)
            p = jnp.exp2(s - m_new)
            m_refs[h][...] = m_new
            if masked:
                produce_scores(h, qsn_refs, 0)
            else:
                produce_scores(h, qst_refs, j + 1)
            acc_refs[h][...] = (acc_refs[h][...] * alpha
                                + jnp.dot(vt_ref[h, j], p.astype(BF16), preferred_element_type=F32))

    n_full = (i * tq) // tk

    @pl.when(i == 0)
    def _():
        for h in range(nh):
            produce_scores(h, qst_refs, 0)
        r_i = lax.broadcasted_iota(jnp.int32, (tk, 2 * tq), 0)
        c_i = lax.broadcasted_iota(jnp.int32, (tk, 2 * tq), 1) % tq
        for par in range(caps):
            cap_ref[par] = jnp.where(r_i - c_i <= par * tq, -NEG_BIG, NEG_BIG)

    def two_steps(jj, c):
        step(2 * jj, False)
        step(2 * jj + 1, False)
        return c
    lax.fori_loop(0, n_full // 2, two_steps, 0)

    @pl.when(n_full % 2 == 1)
    def _():
        step(n_full - 1, False)
    step(n_full, True)

    for h in range(nh):
        o = acc_refs[h][0:DA_VDIM, :] * (1.0 / acc_refs[h][DA_VDIM:DA_VDIM + 1, :])
        od = o[:, :tq] - lam * o[:, tq:]
        y = od * lax.rsqrt(jnp.mean(od * od, axis=0, keepdims=True) + EPS) * g_ref[...] * (1.0 - lambda_init)
        o_ref[:, h * LANES:(h + 1) * LANES] = y.T.astype(BF16)


def _attention(lam_params, subln_col, qt5, k2d, vt5, batch, seq, lambda_init):
    tq, tk = ATT_TQ, ATT_TK
    nq, nk = seq // tq, seq // tk
    nh = DA_HEADS
    body = functools.partial(_attn_body, tq=tq, tk=tk, lambda_init=lambda_init)

    def next_q(b, i):
        f = jnp.minimum(b * nq + i + 1, batch * nq - 1)
        return (f // nq, 0, f % nq, 0, 0)
    return pl.pallas_call(
        body,
        grid=(batch, nq),
        in_specs=[
            pl.BlockSpec(lam_params.shape, lambda b, i: (0, 0)),
            pl.BlockSpec(subln_col.shape, lambda b, i: (0, 0)),
            pl.BlockSpec((None, nh, None, DA_VDIM, tq), lambda b, i: (b, 0, i, 0, 0)),
            pl.BlockSpec((None, nh, None, DA_VDIM, tq), next_q),
            pl.BlockSpec((seq, DA_WIDTH), lambda b, i: (b, 0)),
            pl.BlockSpec((None, nh, nk, V_ROWS, tk), lambda b, i: (b, 0, 0, 0, 0)),
        ],
        out_specs=pl.BlockSpec((tq, DA_WIDTH), lambda b, i: (b * nq + i, 0)),
        out_shape=jax.ShapeDtypeStruct((batch * seq, DA_WIDTH), BF16),
        scratch_shapes=(
            [pltpu.VMEM((DA_VDIM, 2 * tq), BF16)] * (2 * nh)
            + [pltpu.VMEM((1, 2 * tq), F32)] * nh
            + [pltpu.VMEM((V_ROWS, 2 * tq), F32)] * nh
            + [pltpu.VMEM((tk, 2 * tq), F32)] * nh
            + [pltpu.VMEM((1, 2 * tq), F32)] * nh
            + [pltpu.VMEM((tk // tq, tk, 2 * tq), F32)]),
        compiler_params=pltpu.CompilerParams(
            dimension_semantics=("arbitrary", "arbitrary"), vmem_limit_bytes=VMEM_LIMIT),
        name="diff_attention",
    )(lam_params, subln_col, qt5, qt5, k2d, vt5)


RT_GATE = 0
RT_EXPERT = 2
RT_RANK = 4
ROUTER_EXPERT_LANE0 = N_GROUPS


TILE_ROWS = 8


def _to_row_tiles(ref, value):
    rows = value.shape[0]
    for c in range(TILE_ROWS):
        ref[pl.ds(c, rows, stride=TILE_ROWS), :] = value[:, c * LANES:(c + 1) * LANES]


def _from_row_tiles(ref, row0, rows):
    return jnp.concatenate(
        [ref[pl.ds(row0 * TILE_ROWS + c, rows, stride=TILE_ROWS), :] for c in range(TILE_ROWS)], axis=1)


def _tile_of(ref, row):
    return ref.at[pl.ds(pl.multiple_of(row * TILE_ROWS, TILE_ROWS), TILE_ROWS)]


def _out_proj_body(x_ref, oda_ref, ogm_ref, w_ref, g_ref, wr_ref, x1_ref, h2_ref, rt_ref, cnt_ref, carry_ref):
    tm = x_ref.shape[0]

    @pl.when(pl.program_id(0) == 0)
    def _():
        carry_ref[...] = jnp.zeros(carry_ref.shape, F32)

    x1 = (x_ref[...]
          + jnp.dot(oda_ref[...], w_ref[0:DA_WIDTH, :], preferred_element_type=F32)
          + jnp.dot(ogm_ref[...], w_ref[DA_WIDTH:DA_WIDTH + GM_WIDTH, :], preferred_element_type=F32))
    x1_ref[...] = x1
    h2 = _rms(x1, g_ref[...])
    _to_row_tiles(h2_ref, h2)
    lg = jnp.dot(h2.astype(BF16), wr_ref[...], preferred_element_type=F32)

    lane = lax.broadcasted_iota(jnp.int32, lg.shape, 1).astype(F32)
    ninf = -jnp.inf
    first_lane = lambda hit: jnp.min(jnp.where(hit, lane, float(LANES)), axis=-1, keepdims=True)
    is_grp = lane < N_GROUPS
    gl = jnp.where(is_grp, lg, ninf)
    gmax = jnp.max(gl, axis=-1, keepdims=True)
    g_idx = first_lane(gl == gmax)
    g_gate = 1.0 / jnp.sum(jnp.where(is_grp, jnp.exp(lg - gmax), 0.0), axis=-1, keepdims=True)
    e_of_lane = lane - ROUTER_EXPERT_LANE0
    e_lo = g_idx * EXPERTS_PER_GROUP
    in_grp = (e_of_lane >= e_lo) & (e_of_lane < e_lo + EXPERTS_PER_GROUP)
    el = jnp.where(in_grp, lg, ninf)
    v1 = jnp.max(el, axis=-1, keepdims=True)
    i1 = first_lane(el == v1)
    el2 = jnp.where(lane == i1, ninf, el)
    v2 = jnp.max(el2, axis=-1, keepdims=True)
    i2 = first_lane(el2 == v2)
    t21 = jnp.exp(v2 - v1)
    w1 = g_gate / (1.0 + t21)
    w2 = g_gate * t21 / (1.0 + t21)

    hit1 = lane == i1
    hit2 = lane == i2
    chosen = (hit1 | hit2).astype(BF16)
    r_i = lax.broadcasted_iota(jnp.int32, (tm, tm), 0)
    c_i = lax.broadcasted_iota(jnp.int32, (tm, tm), 1)
    before = jnp.where(r_i > c_i, 1.0, 0.0).astype(BF16)
    base = carry_ref[...] + jnp.dot(before, chosen, preferred_element_type=F32)
    rank1 = jnp.sum(jnp.where(hit1, base, 0.0), axis=-1, keepdims=True)
    rank2 = jnp.sum(jnp.where(hit2, base, 0.0), axis=-1, keepdims=True)
    carry_ref[...] = carry_ref[...] + jnp.sum(chosen.astype(F32), axis=0, keepdims=True)
    cnt_ref[...] = carry_ref[...]

    rec = jnp.zeros(lg.shape, F32)
    for k, val in ((RT_GATE, w1), (RT_GATE + 1, w2),
                   (RT_EXPERT, (i1 - ROUTER_EXPERT_LANE0).astype(F32)),
                   (RT_EXPERT + 1, (i2 - ROUTER_EXPERT_LANE0).astype(F32)),
                   (RT_RANK, rank1), (RT_RANK + 1, rank2)):
        rec = jnp.where(lane == k, val, rec)
    rt_ref[...] = rec


def _out_proj(x2d, o_da, o_gm, w_out_b, moe_norm, w_router_b):
    t, d = x2d.shape
    tm = ROW_TILE
    row = lambda i: (i, 0)
    full2 = lambda i: (0, 0)
    return pl.pallas_call(
        _out_proj_body,
        grid=(t // tm,),
        in_specs=[
            pl.BlockSpec((tm, d), row),
            pl.BlockSpec((tm, DA_WIDTH), row),
            pl.BlockSpec((tm, GM_WIDTH), row),
            pl.BlockSpec(w_out_b.shape, full2),
            pl.BlockSpec((1, d), full2),
            pl.BlockSpec(w_router_b.shape, full2),
        ],
        out_specs=[
            pl.BlockSpec((tm, d), row),
            pl.BlockSpec((tm * TILE_ROWS, LANES), row),
            pl.BlockSpec((tm, LANES), row),
            pl.BlockSpec((1, LANES), full2),
        ],
        out_shape=[
            jax.ShapeDtypeStruct((t, d), F32),
            jax.ShapeDtypeStruct((t * TILE_ROWS, LANES), F32),
            jax.ShapeDtypeStruct((t, LANES), F32),
            jax.ShapeDtypeStruct((1, LANES), F32),
        ],
        scratch_shapes=[pltpu.VMEM((1, LANES), F32)],
        compiler_params=pltpu.CompilerParams(dimension_semantics=("arbitrary",), vmem_limit_bytes=VMEM_LIMIT),
        name="out_proj",
    )(x2d, o_da, o_gm, w_out_b, moe_norm, w_router_b)


DMA_GROUP = 8


ZERO_RUN = 32


def _dispatch_body(fill_lo_ref, fill_hi_ref, n_used_ref, dst_ref, h2_ref, buf_hbm, stage, zblk, sem, zsem, rsem,
                   bsem, *, tm, n_blocks):
    i = pl.program_id(0)
    n = pl.num_programs(0)
    slot = i % 2
    stage[slot] = h2_ref[...]

    def row_copy(r, dst, s):
        return pltpu.make_async_copy(_tile_of(stage.at[s], r), _tile_of(buf_hbm, dst), sem.at[s])

    def issue(g, c):
        r0 = g * DMA_GROUP
        dst = [[dst_ref[0, 0, k * tm + r0 + u] for k in range(TOP_K)] for u in range(DMA_GROUP)]
        for u in range(DMA_GROUP):
            for k in range(TOP_K):
                row_copy(r0 + u, dst[u][k], slot).start(priority=k % 2)
        return c
    lax.fori_loop(0, tm // DMA_GROUP, issue, 0)

    def wait_tile(s):
        whole = pltpu.make_async_copy(stage.at[s], buf_hbm.at[pl.ds(0, stage.shape[1])], sem.at[s])
        for _ in range(TOP_K):
            whole.wait()

    @pl.when(i > 0)
    def _():
        wait_tile(1 - slot)

    @pl.when(i == n - 1)
    def _():
        wait_tile(slot)
        zblk[...] = jnp.zeros(zblk.shape, F32)
        fill = lambda r: pltpu.make_async_copy(zblk.at[pl.ds(0, TILE_ROWS)], _tile_of(buf_hbm, r), zsem)
        run_tiles = ZERO_RUN * TILE_ROWS
        fill_run = lambda r0: pltpu.make_async_copy(
            zblk.at[pl.ds(0, run_tiles)],
            buf_hbm.at[pl.ds(pl.multiple_of(r0 * TILE_ROWS, TILE_ROWS), run_tiles)], rsem)
        blk_tiles = zblk.shape[0]
        fill_blk = lambda b: pltpu.make_async_copy(
            zblk, buf_hbm.at[pl.ds(pl.multiple_of(b * blk_tiles, blk_tiles), blk_tiles)], bsem)

        def do(copy_of, method):
            def body(r, c):
                getattr(copy_of(r), method)()
                return c
            return body
        for method in ("start", "wait"):
            for e in range(N_EXPERTS):
                lo, hi = fill_lo_ref[e], fill_hi_ref[e]
                runs = lax.shift_right_logical(hi - lo, ZERO_RUN.bit_length() - 1)
                lax.fori_loop(0, runs, do(lambda c, hi=hi: fill_run(hi - (c + 1) * ZERO_RUN), method), 0)
                lax.fori_loop(lo, hi - runs * ZERO_RUN, do(fill, method), 0)
            lax.fori_loop(n_used_ref[0], n_blocks, do(fill_blk, method), 0)


def _dispatch(fill_lo, fill_hi, n_used, dest_tiles, h2_tiles, n_rows):
    t = h2_tiles.shape[0] // TILE_ROWS
    tm = ROW_TILE
    grid_spec = pltpu.PrefetchScalarGridSpec(
        num_scalar_prefetch=3,
        grid=(t // tm,),
        in_specs=[
            pl.BlockSpec((1, 1, TOP_K * tm), lambda i, lo, hi, nu: (i, 0, 0), memory_space=pltpu.SMEM),
            pl.BlockSpec((tm * TILE_ROWS, LANES), lambda i, lo, hi, nu: (i, 0)),
        ],
        out_specs=pl.BlockSpec(memory_space=pl.ANY),
        scratch_shapes=[
            pltpu.VMEM((2, tm * TILE_ROWS, LANES), F32),
            pltpu.VMEM((MOE_ROWS * TILE_ROWS, LANES), F32),
            pltpu.SemaphoreType.DMA((2,)),
            pltpu.SemaphoreType.DMA,
            pltpu.SemaphoreType.DMA,
            pltpu.SemaphoreType.DMA,
        ],
    )
    return pl.pallas_call(
        functools.partial(_dispatch_body, tm=tm, n_blocks=n_rows // MOE_ROWS),
        grid_spec=grid_spec,
        out_shape=jax.ShapeDtypeStruct((n_rows * TILE_ROWS, LANES), F32),
        compiler_params=pltpu.CompilerParams(dimension_semantics=("arbitrary",), vmem_limit_bytes=VMEM_LIMIT),
        name="dispatch",
    )(fill_lo, fill_hi, n_used, dest_tiles, h2_tiles)


def _moe_body(blk_exp_ref, x_ref, wg_ref, wu_ref, wd_ref, y_ref, wg_b, wu_b, wd_b):
    i = pl.program_id(0)

    @pl.when((i == 0) | (blk_exp_ref[i] != blk_exp_ref[jnp.maximum(i - 1, 0)]))
    def _():
        wg_b[...] = wg_ref[0].astype(BF16)
        wu_b[...] = wu_ref[0].astype(BF16)
        wd_b[...] = wd_ref[0].astype(BF16)

    xb = _from_row_tiles(x_ref, 0, MOE_ROWS).astype(BF16)
    g = jnp.dot(xb, wg_b[...], preferred_element_type=F32)
    u = jnp.dot(xb, wu_b[...], preferred_element_type=F32)
    act = (g * jax.nn.sigmoid(g) * u).astype(BF16)
    _to_row_tiles(y_ref, jnp.dot(act, wd_b[...], preferred_element_type=F32))


def _moe_ffn(blk_exp, buf, wg, wu, wd):
    n_rows = buf.shape[0] // TILE_ROWS
    d, f = wg.shape[1:]
    rows = MOE_ROWS
    row_blk = lambda i, be: (i, 0)
    w_blk = lambda i, be: (be[i], 0, 0)
    grid_spec = pltpu.PrefetchScalarGridSpec(
        num_scalar_prefetch=1,
        grid=(n_rows // rows,),
        in_specs=[
            pl.BlockSpec((rows * TILE_ROWS, LANES), row_blk),
            pl.BlockSpec((1, d, f), w_blk),
            pl.BlockSpec((1, d, f), w_blk),
            pl.BlockSpec((1, f, d), w_blk),
        ],
        out_specs=pl.BlockSpec((rows * TILE_ROWS, LANES), row_blk),
        scratch_shapes=[pltpu.VMEM((d, f), BF16), pltpu.VMEM((d, f), BF16), pltpu.VMEM((f, d), BF16)],
    )
    return pl.pallas_call(
        _moe_body,
        grid_spec=grid_spec,
        out_shape=jax.ShapeDtypeStruct(buf.shape, F32),
        compiler_params=pltpu.CompilerParams(dimension_semantics=("arbitrary",), vmem_limit_bytes=VMEM_LIMIT),
        name="moe_ffn",
    )(blk_exp, buf, wg, wu, wd)


GATHER_AHEAD = 1


def _finish_body(*refs, last_layer):
    src_refs = refs[:GATHER_AHEAD + 1]
    x1_ref, rt_ref, p_ref, pn_ref, wg_ref, bg_ref, wp_ref, fn_ref, y_hbm, o_ref, ybuf, sem = refs[GATHER_AHEAD + 1:]
    tm = x1_ref.shape[0]
    i = pl.program_id(0)
    n = pl.num_programs(0)
    slots = GATHER_AHEAD + 1
    slot = lax.rem(i, slots)

    def row_copy(src, r, s):
        return pltpu.make_async_copy(_tile_of(y_hbm, src), _tile_of(ybuf.at[s], r), sem.at[s])

    def gather_group(idx_ref, s, r0):
        src = [idx_ref[0, 0, r0 + u] for u in range(TOP_K * DMA_GROUP)]
        for u in range(TOP_K * DMA_GROUP):
            row_copy(src[u], r0 + u, s).start(priority=u % 2)

    @pl.when(i == 0)
    def _():
        for a in range(GATHER_AHEAD):
            def body(g, c, a=a):
                gather_group(src_refs[a], a, g * (TOP_K * DMA_GROUP))
                return c
            lax.fori_loop(0, tm // DMA_GROUP, body, 0)

    def wait_slot(s):
        pltpu.make_async_copy(y_hbm.at[pl.ds(0, ybuf.shape[1])], ybuf.at[s], sem.at[s]).wait()

    wait_slot(slot)

    ahead_slot = lax.rem(i + GATHER_AHEAD, slots)
    for g in range(tm // DMA_GROUP):
        gather_group(src_refs[GATHER_AHEAD], ahead_slot, g * (TOP_K * DMA_GROUP))

    rt = rt_ref[...]
    y0 = _from_row_tiles(ybuf.at[slot], 0, tm)
    y1 = _from_row_tiles(ybuf.at[slot], tm, tm)
    x2 = x1_ref[...] + rt[:, RT_GATE:RT_GATE + 1] * y0 + rt[:, RT_GATE + 1:RT_GATE + 2] * y1
    hg = _rms(x2, pn_ref[...]).astype(BF16)
    gate = jax.nn.sigmoid(jnp.dot(hg, wg_ref[...], preferred_element_type=F32) + bg_ref[...])
    x3 = x2 + gate * jnp.dot(p_ref[...].astype(BF16), wp_ref[...], preferred_element_type=F32)
    o_ref[...] = _rms(x3, fn_ref[...]) if last_layer else x3

    @pl.when(i == n - 1)
    def _():
        for a in range(1, GATHER_AHEAD + 1):
            wait_slot(lax.rem(i + a, slots))


def _finish(dest_tiles, x1, rt, p2d, ple_norm, w_gate_b, b_gate, w_proj_b, final_norm, y_tiles, last_layer):
    t, d = x1.shape
    tm = ROW_TILE
    n_tiles = t // tm
    row = lambda i: (i, 0)
    full2 = lambda i: (0, 0)
    return pl.pallas_call(
        functools.partial(_finish_body, last_layer=last_layer),
        grid=(n_tiles,),
        in_specs=[
            pl.BlockSpec((1, 1, TOP_K * tm), lambda i, a=a: (jnp.minimum(i + a, n_tiles - 1), 0, 0),
                         memory_space=pltpu.SMEM)
            for a in range(GATHER_AHEAD + 1)
        ] + [
            pl.BlockSpec((tm, d), row),
            pl.BlockSpec((tm, LANES), row),
            pl.BlockSpec((tm, p2d.shape[1]), row),
            pl.BlockSpec((1, d), full2),
            pl.BlockSpec(w_gate_b.shape, full2),
            pl.BlockSpec((1, d), full2),
            pl.BlockSpec(w_proj_b.shape, full2),
            pl.BlockSpec((1, d), full2),
            pl.BlockSpec(memory_space=pl.ANY),
        ],
        out_specs=pl.BlockSpec((tm, d), row),
        out_shape=jax.ShapeDtypeStruct((t, d), F32),
        scratch_shapes=[
            pltpu.VMEM((GATHER_AHEAD + 1, TOP_K * tm * TILE_ROWS, LANES), F32),
            pltpu.SemaphoreType.DMA((GATHER_AHEAD + 1,)),
        ],
        compiler_params=pltpu.CompilerParams(dimension_semantics=("arbitrary",), vmem_limit_bytes=VMEM_LIMIT),
        name="finish",
    )(*([dest_tiles] * (GATHER_AHEAD + 1)), x1, rt, p2d, ple_norm, w_gate_b, b_gate, w_proj_b, final_norm, y_tiles)


def _qk_column_perm():
    perm = np.zeros((DA_WIDTH,), np.int32)
    for h in range(DA_HEADS):
        for half in range(2):
            for mp in range(2):
                for i in range(ROPE_HALF):
                    perm[h * LANES + half * 64 + mp * ROPE_HALF + i] = h * LANES + mp * DA_HEAD_DIM + half * ROPE_HALF + i
    return perm


def _rope_tables(seq):
    inv = 1.0 / (ROPE_THETA ** (jnp.arange(0, DA_HEAD_DIM, 2, dtype=F32) / DA_HEAD_DIM))
    ang = jnp.arange(seq, dtype=F32)[:, None] * inv[None, :]
    cos = jnp.tile(jnp.cos(ang), (1, LANES // ROPE_HALF))
    sin = jnp.tile(jnp.sin(ang), (1, LANES // ROPE_HALF))
    sign = jnp.where(jnp.arange(LANES) < LANES // 2, -1.0, 1.0).astype(F32)
    return cos, sin * sign[None, :]


def _dispatch_tables(rt, counts_f, n_tokens):
    rows = MOE_ROWS
    tm = ROW_TILE
    counts = counts_f[0, ROUTER_EXPERT_LANE0:ROUTER_EXPERT_LANE0 + N_EXPERTS].astype(jnp.int32)
    padded = ((counts + rows - 1) // rows) * rows
    pend = jnp.cumsum(padded)
    pstart = pend - padded
    n_blocks = (n_tokens * TOP_K + N_EXPERTS * rows) // rows
    n_used = (pend[-1:] // rows).astype(jnp.int32)
    blk_start = jnp.arange(n_blocks, dtype=jnp.int32) * rows
    blk_exp = jnp.minimum(jnp.sum(pend[None, :] <= blk_start[:, None], axis=1), N_EXPERTS - 1).astype(jnp.int32)
    expert = rt[:, RT_EXPERT:RT_EXPERT + TOP_K].astype(jnp.int32)
    rank = rt[:, RT_RANK:RT_RANK + TOP_K].astype(jnp.int32)
    first_row = jnp.sum(jnp.where(expert[..., None] == jnp.arange(N_EXPERTS), pstart, 0), axis=-1)
    dest = first_row + rank
    dest_tiles = dest.reshape(n_tokens // tm, tm, TOP_K).transpose(0, 2, 1).reshape(n_tokens // tm, 1, TOP_K * tm)
    return blk_exp, n_used, (pstart + counts).astype(jnp.int32), pend.astype(jnp.int32), dest_tiles, n_blocks * rows


def _layer(x, p_i, i, last_layer, final_norm, attn_norm, w_in, lambda_q1, lambda_k1, lambda_q2, lambda_k2,
           diff_subln, gm_ln_gain, gm_ln_bias, gm_spatial_w, gm_spatial_b, gm_out_norm, w_out, moe_norm,
           w_group_router, w_expert_router, w_expert_gate, w_expert_up, w_expert_down, ple_norm, w_ple_gate,
           b_ple_gate, w_ple_proj):
    batch, seq, d = x.shape
    t = batch * seq
    lambda_init = 0.8 - 0.6 * math.exp(-0.3 * i)
    x2d = x.reshape(t, d)

    perm = _qk_column_perm()
    w_in_b = jnp.concatenate(
        [w_in[:, perm], w_in[:, DA_WIDTH + perm], w_in[:, 2 * DA_WIDTH:]], axis=1).astype(BF16)
    cos_t, sin_t = _rope_tables(seq)
    qt5, k, vt5, o_gm = _in_proj(x2d, attn_norm[None, :], w_in_b, cos_t, sin_t, gm_ln_gain[None, :],
                                 gm_ln_bias[None, :], gm_spatial_w, gm_spatial_b.T, gm_out_norm[None, :], seq)
    lam_params = jnp.stack([lambda_q1, lambda_k1, lambda_q2, lambda_k2]).astype(F32)
    o_da = _attention(lam_params, diff_subln[:, None], qt5, k, vt5, batch, seq, lambda_init)

    w_router = jnp.concatenate([w_group_router, w_expert_router], axis=1)
    w_router_b = jnp.pad(w_router, ((0, 0), (0, LANES - w_router.shape[1]))).astype(BF16)
    x1, h2_tiles, rt, counts_f = _out_proj(x2d, o_da, o_gm, w_out.astype(BF16), moe_norm[None, :], w_router_b)

    blk_exp, n_used, fill_lo, fill_hi, dest_tiles, n_rows = _dispatch_tables(rt, counts_f, t)
    buf = _dispatch(fill_lo, fill_hi, n_used, dest_tiles, h2_tiles, n_rows)
    y_tiles = _moe_ffn(blk_exp, buf, w_expert_gate, w_expert_up, w_expert_down)

    out = _finish(dest_tiles, x1, rt, p_i.reshape(t, -1), ple_norm[None, :], w_ple_gate.astype(BF16),
                  b_ple_gate[None, :], w_ple_proj.astype(BF16), final_norm[None, :], y_tiles, last_layer)
    return out.reshape(batch, seq, d)


def kernel(x, p, attn_norm, w_in, lambda_q1, lambda_k1, lambda_q2, lambda_k2, diff_subln, gm_ln_gain, gm_ln_bias, gm_spatial_w, gm_spatial_b, gm_out_norm, w_out, moe_norm, w_group_router, w_expert_router, w_expert_gate, w_expert_up, w_expert_down, ple_norm, w_ple_gate, b_ple_gate, w_ple_proj, final_norm):
    per_layer = (attn_norm, w_in, lambda_q1, lambda_k1, lambda_q2, lambda_k2, diff_subln, gm_ln_gain, gm_ln_bias,
                 gm_spatial_w, gm_spatial_b, gm_out_norm, w_out, moe_norm, w_group_router, w_expert_router,
                 w_expert_gate, w_expert_up, w_expert_down, ple_norm, w_ple_gate, b_ple_gate, w_ple_proj)
    depth = attn_norm.shape[0]
    for i in range(depth):
        x = _layer(x, p[i], i, i == depth - 1, final_norm, *(w[i] for w in per_layer))
    return x
```
